```python
import math
import jax
import jax.numpy as jnp
from jax import lax
import numpy as np

D_MODEL = 4096
BATCH = 4
SEQ = 2048
DEPTH = 4
DEC_BATCH = 8
DEC_SEQ = 1
PAST_LEN = 8192
PAGE_SIZE = 128

N_MIXERS = 4
D_FF = 4 * D_MODEL
NORM_EPS = 1e-6
NEG_INF = -1e30
TINY = 1e-30

M_HEADS = 8
M_DV = D_MODEL // M_HEADS
M_DQK = M_DV // 2
M_CHUNK = 64
M_GATE_CAP = 15.0

R_HEAD = 64
R_HEADS = D_MODEL // R_HEAD
R_DECAY_LORA = 128
R_A_LORA = 128
R_GATE_LORA = 480
R_GN_EPS = 64e-5

CONV_W = 3

A_HEADS = 32
A_HD = D_MODEL // A_HEADS
A_KV = 4
A_HPG = A_HEADS // A_KV
ROPE_DIM = A_HD // 4
ROPE_THETA = 500000.0
CMP_LEN = 32
CMP_STRIDE = 16
CMP_HID = 256
SEL_BLK = 64
SEL_TOP = 16
WINDOW = 512
NSA_QBLK = 64
FORCE_SCORE = 1e9

kernel_name = 'hybrid_mlstm_rwkv7_shortconv_nsa_step'


def rms_norm(x, g):
    xf = x.astype(jnp.float32)
    y = xf * lax.rsqrt(jnp.mean(xf * xf, axis=-1, keepdims=True) + NORM_EPS)
    return (y * g.astype(jnp.float32)).astype(x.dtype)


def sq_relu_mlp(x, w_up, w_down):
    h = jax.nn.relu(x @ w_up)
    return (h * h) @ w_down


def masked_softmax(s, mask):
    s = jnp.where(mask, s.astype(jnp.float32), NEG_INF)
    m = jnp.max(s, axis=-1, keepdims=True)
    e = jnp.where(mask, jnp.exp(s - m), 0.0)
    return e / jnp.maximum(jnp.sum(e, axis=-1, keepdims=True), TINY)


def partial_rope(x, pos):
    half = ROPE_DIM // 2
    inv = ROPE_THETA ** (-jnp.arange(half, dtype=jnp.float32) / half)
    ang = pos.astype(jnp.float32)[:, None] * inv[None, :]
    cos, sin = jnp.cos(ang)[:, None, :], jnp.sin(ang)[:, None, :]
    xr = x[..., :ROPE_DIM].astype(jnp.float32)
    x1, x2 = xr[..., :half], xr[..., half:]
    rot = jnp.concatenate([x1 * cos - x2 * sin, x2 * cos + x1 * sin], axis=-1).astype(x.dtype)
    return jnp.concatenate([rot, x[..., ROPE_DIM:]], axis=-1)


def _softcap(z):
    return M_GATE_CAP * jnp.tanh(z / M_GATE_CAP)


def mlstm_mixer(x, C0, n0, m0, w_in, b_gate, g_norm, w_out):
    B, T, _ = x.shape
    f32 = jnp.float32
    qd, vd = M_HEADS * M_DQK, M_HEADS * M_DV
    q, k, v, o, ig, fg = jnp.split(x @ w_in, [qd, 2 * qd, 2 * qd + vd, 2 * qd + 2 * vd, 2 * qd + 2 * vd + M_HEADS], axis=-1)
    q = q.reshape(B, T, M_HEADS, M_DQK).astype(f32)
    k = k.reshape(B, T, M_HEADS, M_DQK).astype(f32) * (M_DQK ** -0.5)
    v = v.reshape(B, T, M_HEADS, M_DV).astype(f32)
    log_i = _softcap((ig + b_gate[0]).astype(f32))
    log_f = jax.nn.log_sigmoid(_softcap((fg + b_gate[1]).astype(f32)))
    L = math.gcd(T, M_CHUNK)
    nc = T // L

    def chunks(a):
        a = a.reshape(B, nc, L, M_HEADS, *a.shape[3:])
        return jnp.moveaxis(a, (1, 3), (0, 2))

    causal = jnp.tril(jnp.ones((L, L), dtype=bool))

    def step(carry, inp):
        C, n, m = carry
        qc, kc, vc, li, lf = inp
        b = jnp.cumsum(lf, axis=-1)
        dlog = jnp.where(causal, b[..., :, None] - b[..., None, :] + li[..., None, :], -jnp.inf)
        m_inter = b + m[..., None]
        m_s = jnp.maximum(m_inter, jnp.max(dlog, axis=-1))
        a = jnp.exp(dlog - m_s[..., None]) * jnp.einsum('bhsd,bhrd->bhsr', qc, kc)
        w_inter = jnp.exp(m_inter - m_s)
        num = w_inter[..., None] * jnp.einsum('bhsd,bhde->bhse', qc, C) + jnp.einsum('bhsr,bhre->bhse', a, vc)
        den = w_inter * jnp.einsum('bhsd,bhd->bhs', qc, n) + jnp.sum(a, axis=-1)
        h = num / jnp.maximum(jnp.abs(den), jnp.exp(-m_s))[..., None]
        b_last = b[..., -1]
        g_r = b_last[..., None] - b + li
        m_new = jnp.maximum(b_last + m, jnp.max(g_r, axis=-1))
        w_r = jnp.exp(g_r - m_new[..., None])
        w_c = jnp.exp(b_last + m - m_new)
        C_new = w_c[..., None, None] * C + jnp.einsum('bhr,bhrd,bhre->bhde', w_r, kc, vc)
        n_new = w_c[..., None] * n + jnp.einsum('bhr,bhrd->bhd', w_r, kc)
        return (C_new, n_new, m_new), h

    (C, n, m), h = lax.scan(step, (C0.astype(f32), n0.astype(f32), m0.astype(f32)),
                            (chunks(q), chunks(k), chunks(v), chunks(log_i), chunks(log_f)))
    h = jnp.moveaxis(h, (0, 2), (1, 3)).reshape(B, T, M_HEADS, M_DV)
    h = h * lax.rsqrt(jnp.mean(h * h, axis=-1, keepdims=True) + NORM_EPS) * g_norm
    h = h * jax.nn.sigmoid(o.reshape(B, T, M_HEADS, M_DV).astype(f32))
    y = h.reshape(B, T, M_HEADS * M_DV).astype(x.dtype) @ w_out
    return y, (C.astype(C0.dtype), n.astype(n0.dtype), m.astype(m0.dtype))


def rwkv7_mixer(x, S0, shift0, mix, vec, w_rkv, w1, w2, a1, a2, g1, g2, wo):
    B, T, D = x.shape
    f32 = jnp.float32
    xx = jnp.concatenate([shift0[:, None, :].astype(x.dtype), x[:, :-1]], axis=1) - x

    def lerp(j):
        return x + xx * mix[j]

    w0, a0, k_k, k_a, r_k, gn_w, gn_b = vec[0], vec[1], vec[2], vec[3], vec[4], vec[5], vec[6]
    r = lerp(0) @ w_rkv[0]
    k = lerp(2) @ w_rkv[1]
    v = lerp(3) @ w_rkv[2]
    w_log = -jax.nn.softplus(-(w0 + jnp.tanh(lerp(1) @ w1) @ w2).astype(f32)) - 0.5
    decay = jnp.exp(-jnp.exp(w_log))
    a = jax.nn.sigmoid((a0 + (lerp(4) @ a1) @ a2).astype(f32))
    g = jax.nn.sigmoid(lerp(5) @ g1) @ g2

    def heads(z):
        return z.reshape(B, T, R_HEADS, R_HEAD).astype(f32)

    kk = heads(k * k_k)
    kk = kk / jnp.maximum(jnp.sqrt(jnp.sum(kk * kk, axis=-1, keepdims=True)), 1e-12)
    k = k.astype(f32) * (1.0 + (a - 1.0) * k_a)
    r_h, k_h, v_h, a_h, w_h = heads(r), heads(k), heads(v), heads(a), heads(decay)

    def step(S, inp):
        r_t, w_t, k_t, v_t, kk_t, a_t = inp
        sa = jnp.einsum('bhvk,bhk->bhv', S, -kk_t)
        S = S * w_t[:, :, None, :] + sa[..., None] * (kk_t * a_t)[:, :, None, :] + v_t[..., None] * k_t[:, :, None, :]
        return S, jnp.einsum('bhvk,bhk->bhv', S, r_t)

    S, y = lax.scan(step, S0.astype(f32),
                    (jnp.moveaxis(r_h, 1, 0), jnp.moveaxis(w_h, 1, 0), jnp.moveaxis(k_h, 1, 0),
                     jnp.moveaxis(v_h, 1, 0), jnp.moveaxis(kk, 1, 0), jnp.moveaxis(a_h, 1, 0)))
    y = jnp.moveaxis(y, 0, 1)
    mu = jnp.mean(y, axis=-1, keepdims=True)
    var = jnp.mean((y - mu) ** 2, axis=-1, keepdims=True)
    y = ((y - mu) * lax.rsqrt(var + R_GN_EPS)).reshape(B, T, D) * gn_w + gn_b
    bonus = jnp.sum(r_h * k_h * r_k.reshape(R_HEADS, R_HEAD), axis=-1, keepdims=True) * v_h
    y = (y + bonus.reshape(B, T, D)) * g
    return y.astype(x.dtype) @ wo, (S.astype(S0.dtype), x[:, -1])


def shortconv_mixer(x, buf, w_in, conv_w, w_out):
    T = x.shape[1]
    b_gate, c_gate, u = jnp.split(x @ w_in, 3, axis=-1)
    z = jnp.concatenate([buf.astype(x.dtype), c_gate * u], axis=1)
    conv = lax.conv_general_dilated(z, conv_w[:, None, :].astype(z.dtype), window_strides=(1,), padding='VALID',
                                    dimension_numbers=('NWC', 'WIO', 'NWC'), feature_group_count=D_MODEL)
    return (b_gate * conv) @ w_out, z[:, T:]


def nsa_project(x, pos, w_in):
    B, T, _ = x.shape
    qd, kvd = A_HEADS * A_HD, 6 * A_KV * A_HD
    q, kv, gate = jnp.split(x @ w_in, [qd, qd + kvd], axis=-1)
    q = q.reshape(B, T, A_HEADS, A_HD)
    kv = kv.reshape(B, T, 6, A_KV, A_HD)
    rows = jnp.stack([kv[:, :, 0], kv[:, :, 1], partial_rope(kv[:, :, 2], pos), kv[:, :, 3]], axis=2)
    win = jnp.stack([partial_rope(kv[:, :, 4], pos), kv[:, :, 5]], axis=2)
    gate = jax.nn.sigmoid(gate.reshape(B, T, A_HEADS, 3).astype(jnp.float32))
    return q, partial_rope(q, pos), rows, win, gate


def nsa_compress(kv_cmp, pe, w1, w2):
    B, T = kv_cmp.shape[:2]
    nc = (T - CMP_LEN) // CMP_STRIDE + 1
    idx = (jnp.arange(nc) * CMP_STRIDE)[:, None] + jnp.arange(CMP_LEN)[None, :]
    blk = kv_cmp[:, idx] + jnp.transpose(pe, (1, 0, 2))[:, :, None, :]
    flat = jnp.transpose(blk, (0, 1, 3, 4, 2, 5)).reshape(B, nc, 2, A_KV, CMP_LEN * A_HD)
    hid = jax.nn.gelu(jnp.einsum('bnckf,cfm->bnckm', flat, w1))
    out = jnp.einsum('bnckm,cmd->bnckd', hid, w2)
    return out[:, :, 0], out[:, :, 1]


def nsa_attend(q, q_rot, q_pos, kc, vc, ks, vs, kw, vw, kw_pos, gate):
    B, Qn = q.shape[:2]
    f32 = jnp.float32
    scale = A_HD ** -0.5
    qc = q.reshape(B, Qn, A_KV, A_HPG, A_HD).astype(f32) * scale
    qs = q_rot.reshape(B, Qn, A_KV, A_HPG, A_HD).astype(f32) * scale
    t = q_pos
    NC = kc.shape[1]
    c_start = jnp.arange(NC) * CMP_STRIDE
    c_mask = (c_start + CMP_LEN - 1)[None, :] <= t[:, None]
    p_c = masked_softmax(jnp.einsum('bqghd,bngd->bghqn', qc, kc), c_mask)
    o_c = jnp.einsum('bghqn,bngd->bqghd', p_c, vc)
    TK = ks.shape[1]
    NS = -(-TK // SEL_BLK)

    def blocks(a):
        a = jnp.pad(a, ((0, 0), (0, NS * SEL_BLK - TK), (0, 0), (0, 0)))
        return jnp.transpose(a.reshape(B, NS, SEL_BLK, A_KV, A_HD), (0, 3, 1, 2, 4))

    ks_b, vs_b = blocks(ks), blocks(vs)
    s_start = jnp.arange(NS) * SEL_BLK
    overlap = ((c_start[:, None] < s_start[None, :] + SEL_BLK) &
               (c_start[:, None] + CMP_LEN > s_start[None, :])).astype(f32)
    imp = jnp.einsum('bghqn,ns->bgqs', p_c, overlap)
    cur = (t // SEL_BLK)[:, None]
    j = jnp.arange(NS)[None, :]
    valid = s_start[None, :] <= t[:, None]
    forced = (j == 0) | (j == cur) | (j == cur - 1)
    score = jnp.where(valid, jnp.where(forced, FORCE_SCORE, imp), NEG_INF)
    n_top = min(SEL_TOP, NS)
    _, idx = lax.top_k(score, n_top)
    sel_ok = jnp.take_along_axis(jnp.broadcast_to(valid, score.shape), idx, axis=-1)
    bi = jnp.arange(B)[:, None, None, None]
    gi = jnp.arange(A_KV)[None, :, None, None]
    k_sel = ks_b[bi, gi, idx]
    v_sel = vs_b[bi, gi, idx].reshape(B, A_KV, Qn, n_top * SEL_BLK, A_HD)
    key_pos = idx[..., None] * SEL_BLK + jnp.arange(SEL_BLK)
    s_mask = (key_pos <= t[None, None, :, None, None]) & sel_ok[..., None]
    s_s = jnp.einsum('bqghd,bgqnkd->bghqnk', qs, k_sel).reshape(B, A_KV, A_HPG, Qn, n_top * SEL_BLK)
    p_s = masked_softmax(s_s, s_mask.reshape(B, A_KV, 1, Qn, n_top * SEL_BLK))
    o_s = jnp.einsum('bghqm,bgqmd->bqghd', p_s, v_sel)
    w_mask = ((kw_pos[None, :] <= t[:, None]) & (kw_pos[None, :] > t[:, None] - WINDOW) & (kw_pos[None, :] >= 0))
    p_w = masked_softmax(jnp.einsum('bqghd,bkgd->bghqk', qs, kw), w_mask)
    o_w = jnp.einsum('bghqk,bkgd->bqghd', p_w, vw)
    g = gate.reshape(B, Qn, A_KV, A_HPG, 3)
    o = g[..., 0:1] * o_c + g[..., 1:2] * o_s + g[..., 2:3] * o_w
    return o.reshape(B, Qn, A_HEADS * A_HD)


def nsa_prompt(x, w_in, pe, w1, w2, w_out):
    B, T, _ = x.shape
    pos = jnp.arange(T, dtype=jnp.int32)
    q, q_rot, rows, win, gate = nsa_project(x, pos, w_in)
    kc, vc = nsa_compress(rows[:, :, 0:2], pe, w1, w2)
    ks, vs = rows[:, :, 2], rows[:, :, 3]
    win_pad = jnp.pad(win, ((0, 0), (WINDOW, 0), (0, 0), (0, 0), (0, 0)))
    QB = math.gcd(T, NSA_QBLK)

    def block(b):
        s = b * QB
        wb = lax.dynamic_slice_in_dim(win_pad, s, WINDOW + QB, axis=1)
        return nsa_attend(lax.dynamic_slice_in_dim(q, s, QB, axis=1),
                          lax.dynamic_slice_in_dim(q_rot, s, QB, axis=1),
                          s + jnp.arange(QB, dtype=jnp.int32), kc, vc, ks, vs, wb[:, :, 0], wb[:, :, 1],
                          s - WINDOW + jnp.arange(WINDOW + QB, dtype=jnp.int32),
                          lax.dynamic_slice_in_dim(gate, s, QB, axis=1))

    o = lax.map(block, jnp.arange(T // QB, dtype=jnp.int32))
    o = jnp.transpose(o, (1, 0, 2, 3)).reshape(B, T, A_HEADS * A_HD)
    return o.astype(x.dtype) @ w_out, rows, win[:, T - min(WINDOW, T):]


def nsa_sample(x, cache_kv, page_table, win_buf, w_in, pe, w1, w2, w_out):
    B, S, _ = x.shape
    past = page_table.shape[1] * PAGE_SIZE
    pos = past + jnp.arange(S, dtype=jnp.int32)
    q, q_rot, rows, win, gate = nsa_project(x, pos, w_in)
    past_rows = cache_kv[page_table].reshape(B, past, 4, A_KV, A_HD)
    all_rows = jnp.concatenate([past_rows.astype(rows.dtype), rows], axis=1)
    kc, vc = nsa_compress(all_rows[:, :, 0:2], pe, w1, w2)
    wb_len = win_buf.shape[1]
    kw_all = jnp.concatenate([win_buf.astype(win.dtype), win], axis=1)
    kw_pos = past - wb_len + jnp.arange(wb_len + S, dtype=jnp.int32)
    o = nsa_attend(q, q_rot, pos, kc, vc, all_rows[:, :, 2], all_rows[:, :, 3],
                   kw_all[:, :, 0], kw_all[:, :, 1], kw_pos, gate)
    return o.astype(x.dtype) @ w_out, rows, kw_all[:, S:]


def setup_inputs(seed: int = 0) -> dict:
    key = jax.random.key(seed)
    keys = iter(jax.random.split(key, 64))
    f32 = jnp.float32

    def nrm(shape, scale):
        return scale * jax.random.normal(next(keys), shape, f32)

    def unif(shape, lo, hi):
        return jax.random.uniform(next(keys), shape, f32, lo, hi)

    D = D_MODEL
    n_pages = PAST_LEN // PAGE_SIZE
    n_used = DEC_BATCH * n_pages
    n_pool = n_used + (n_used + 3) // 4
    win_len = min(WINDOW, PAST_LEN)
    page_table = jax.random.permutation(next(keys), n_pool)[:n_used].reshape(DEC_BATCH, n_pages).astype(jnp.int32)
    m_in_width = 2 * M_HEADS * M_DQK + 2 * M_HEADS * M_DV + 2 * M_HEADS
    a_in_width = A_HEADS * A_HD + 6 * A_KV * A_HD + 3 * A_HEADS
    return {
        'x_prompt': nrm((BATCH, SEQ, D), 1.0),
        'x_sample': nrm((DEC_BATCH, DEC_SEQ, D), 1.0),
        'state_mlstm_C': nrm((DEC_BATCH, M_HEADS, M_DQK, M_DV), 0.1),
        'state_mlstm_n': nrm((DEC_BATCH, M_HEADS, M_DQK), 0.1),
        'state_mlstm_m': nrm((DEC_BATCH, M_HEADS), 0.5),
        'state_rwkv_S': nrm((DEC_BATCH, R_HEADS, R_HEAD, R_HEAD), 0.1),
        'state_rwkv_shift': nrm((DEC_BATCH, D), 1.0),
        'state_conv': nrm((DEC_BATCH, CONV_W - 1, D), 1.0),
        'cache_nsa_kv': nrm((n_pool, PAGE_SIZE, 4, A_KV, A_HD), 1.0),
        'state_nsa_win': nrm((DEC_BATCH, win_len, 2, A_KV, A_HD), 1.0),
        'page_table': page_table,
        'norm_g': 1.0 + nrm((DEPTH, 4, D), 0.02),
        'mlp_up': nrm((DEPTH, D, D_FF), D ** -0.5),
        'mlp_down': nrm((DEPTH, D_FF, D), D_FF ** -0.5),
        'm_w_in': nrm((D, m_in_width), D ** -0.5),
        'm_b_gate': jnp.stack([nrm((M_HEADS,), 0.1), 3.0 + nrm((M_HEADS,), 0.1)]),
        'm_norm': 1.0 + nrm((M_HEADS, M_DV), 0.02),
        'm_w_out': nrm((M_HEADS * M_DV, D), (M_HEADS * M_DV) ** -0.5),
        'r_mix': unif((6, D), 0.0, 1.0),
        'r_vec': jnp.stack([unif((D,), -5.0, 1.0), nrm((D,), 0.1), 0.85 + nrm((D,), 0.05),
                            1.0 + nrm((D,), 0.05), -0.04 + nrm((D,), 0.05),
                            1.0 + nrm((D,), 0.02), nrm((D,), 0.02)]),
        'r_w_rkv': nrm((3, D, D), D ** -0.5),
        'r_w1': nrm((D, R_DECAY_LORA), D ** -0.5),
        'r_w2': nrm((R_DECAY_LORA, D), 0.1 * R_DECAY_LORA ** -0.5),
        'r_a1': nrm((D, R_A_LORA), D ** -0.5),
        'r_a2': nrm((R_A_LORA, D), 0.1 * R_A_LORA ** -0.5),
        'r_g1': nrm((D, R_GATE_LORA), D ** -0.5),
        'r_g2': nrm((R_GATE_LORA, D), R_GATE_LORA ** -0.5),
        'r_wo': nrm((D, D), D ** -0.5),
        'c_w_in': nrm((D, 3 * D), D ** -0.5),
        'c_conv': nrm((CONV_W, D), CONV_W ** -0.5),
        'c_w_out': nrm((D, D), D ** -0.5),
        'a_w_in': nrm((D, a_in_width), D ** -0.5),
        'a_cmp_pe': nrm((2, CMP_LEN, A_HD), 0.1),
        'a_cmp_w1': nrm((2, CMP_LEN * A_HD, CMP_HID), (CMP_LEN * A_HD) ** -0.5),
        'a_cmp_w2': nrm((2, CMP_HID, A_HD), CMP_HID ** -0.5),
        'a_w_out': nrm((A_HEADS * A_HD, D), (A_HEADS * A_HD) ** -0.5),
    }


def reference(x_prompt, x_sample, state_mlstm_C, state_mlstm_n, state_mlstm_m, state_rwkv_S, state_rwkv_shift,
              state_conv, cache_nsa_kv, state_nsa_win, page_table, norm_g, mlp_up, mlp_down,
              m_w_in, m_b_gate, m_norm, m_w_out, r_mix, r_vec, r_w_rkv, r_w1, r_w2, r_a1, r_a2, r_g1, r_g2, r_wo,
              c_w_in, c_conv, c_w_out, a_w_in, a_cmp_pe, a_cmp_w1, a_cmp_w2, a_w_out):
    xp, xs = x_prompt, x_sample
    Bp = xp.shape[0]
    dt = xp.dtype
    for i in range(DEPTH):
        kind = i % N_MIXERS
        hp, hs = rms_norm(xp, norm_g[i, 0]), rms_norm(xs, norm_g[i, 0])
        if kind == 0:
            yp, (mC_p, mn_p, mm_p) = mlstm_mixer(hp, jnp.zeros((Bp, M_HEADS, M_DQK, M_DV), dt),
                                                 jnp.zeros((Bp, M_HEADS, M_DQK), dt), jnp.zeros((Bp, M_HEADS), dt),
                                                 m_w_in, m_b_gate, m_norm, m_w_out)
            ys, (mC_s, mn_s, mm_s) = mlstm_mixer(hs, state_mlstm_C, state_mlstm_n, state_mlstm_m,
                                                 m_w_in, m_b_gate, m_norm, m_w_out)
        elif kind == 1:
            yp, (rS_p, rx_p) = rwkv7_mixer(hp, jnp.zeros((Bp, R_HEADS, R_HEAD, R_HEAD), dt), jnp.zeros((Bp, D_MODEL), dt),
                                           r_mix, r_vec, r_w_rkv, r_w1, r_w2, r_a1, r_a2, r_g1, r_g2, r_wo)
            ys, (rS_s, rx_s) = rwkv7_mixer(hs, state_rwkv_S, state_rwkv_shift,
                                           r_mix, r_vec, r_w_rkv, r_w1, r_w2, r_a1, r_a2, r_g1, r_g2, r_wo)
        elif kind == 2:
            yp, cv_p = shortconv_mixer(hp, jnp.zeros((Bp, CONV_W - 1, D_MODEL), dt), c_w_in, c_conv, c_w_out)
            ys, cv_s = shortconv_mixer(hs, state_conv, c_w_in, c_conv, c_w_out)
        else:
            yp, kv_p, win_p = nsa_prompt(hp, a_w_in, a_cmp_pe, a_cmp_w1, a_cmp_w2, a_w_out)
            ys, kv_s, win_s = nsa_sample(hs, cache_nsa_kv, page_table, state_nsa_win,
                                         a_w_in, a_cmp_pe, a_cmp_w1, a_cmp_w2, a_w_out)
        xp = xp + rms_norm(yp, norm_g[i, 1])
        xs = xs + rms_norm(ys, norm_g[i, 1])
        xp = xp + rms_norm(sq_relu_mlp(rms_norm(xp, norm_g[i, 2]), mlp_up[i], mlp_down[i]), norm_g[i, 3])
        xs = xs + rms_norm(sq_relu_mlp(rms_norm(xs, norm_g[i, 2]), mlp_up[i], mlp_down[i]), norm_g[i, 3])
    return (xp, xs, mC_p, mC_s, mn_p, mn_s, mm_p, mm_s, rS_p, rS_s, rx_p, rx_s, cv_p, cv_s, kv_p, kv_s, win_p, win_s)
```

```python
import functools
import math

import jax
import jax.numpy as jnp
from jax import lax
from jax.experimental import pallas as pl
from jax.experimental.pallas import tpu as pltpu

D_MODEL = 4096
N_MIXERS = 4
NORM_EPS = 1e-6
NEG_INF = -1e30
TINY = 1e-30

M_HEADS = 8
M_DV = D_MODEL // M_HEADS
M_DQK = M_DV // 2
M_CHUNK = 64
M_GATE_CAP = 15.0

R_HEAD = 64
R_HEADS = D_MODEL // R_HEAD
R_GN_EPS = 64e-5

CONV_W = 3

A_HEADS = 32
A_HD = D_MODEL // A_HEADS
A_KV = 4
A_HPG = A_HEADS // A_KV
ROPE_DIM = A_HD // 4
ROPE_THETA = 500000.0
CMP_LEN = 32
CMP_STRIDE = 16
SEL_BLK = 64
SEL_TOP = 16
WINDOW = 512
NSA_QBLK = 64
FORCE_SCORE = 1e9
PAGE_SIZE = 128

VMEM_LIMIT_BYTES = 48 * 1024 * 1024


def _mm_kernel(x_ref, w_ref, o_ref, acc_ref, *, nk, epilogue):
    k = pl.program_id(2)

    @pl.when(k == 0)
    def _():
        acc_ref[...] = jnp.zeros_like(acc_ref)

    acc_ref[...] += jnp.dot(x_ref[...].astype(jnp.bfloat16), w_ref[...].astype(jnp.bfloat16),
                            preferred_element_type=jnp.float32)

    @pl.when(k == nk - 1)
    def _():
        acc = acc_ref[...]
        if epilogue == 'relu2':
            r = jnp.maximum(acc, 0.0)
            acc = r * r
        o_ref[...] = acc.astype(o_ref.dtype)


def _pick(n, pref):
    for t in pref:
        if n % t == 0:
            return t
    return n


def mm(x, w, epilogue=None, out_dtype=jnp.float32):
    lead = x.shape[:-1]
    K = x.shape[-1]
    N = w.shape[1]
    x2 = x.reshape(-1, K)
    M = x2.shape[0]
    tm = _pick(M, (1024, 512, 256, 128, 64, 32, 16, 8))
    tn = 512 if N > 1024 else N
    tk = _pick(K, (1024, 512, 256, 128))
    nk = K // tk
    out = pl.pallas_call(
        functools.partial(_mm_kernel, nk=nk, epilogue=epilogue),
        grid=(M // tm, pl.cdiv(N, tn), nk),
        in_specs=[pl.BlockSpec((tm, tk), lambda i, j, k: (i, k)),
                  pl.BlockSpec((tk, tn), lambda i, j, k: (k, j))],
        out_specs=pl.BlockSpec((tm, tn), lambda i, j, k: (i, j)),
        out_shape=jax.ShapeDtypeStruct((M, N), out_dtype),
        scratch_shapes=[pltpu.VMEM((tm, tn), jnp.float32)],
        compiler_params=pltpu.CompilerParams(
            dimension_semantics=("parallel", "parallel", "arbitrary"),
            vmem_limit_bytes=VMEM_LIMIT_BYTES),
        name="mm",
    )(x2, w)
    return out.reshape(*lead, N)


def rms_norm(x, g):
    xf = x.astype(jnp.float32)
    y = xf * lax.rsqrt(jnp.mean(xf * xf, axis=-1, keepdims=True) + NORM_EPS)
    return (y * g.astype(jnp.float32)).astype(x.dtype)


def sq_relu_mlp(x, w_up, w_down):
    h = mm(x, w_up, epilogue='relu2', out_dtype=jnp.bfloat16)
    return mm(h, w_down)


def masked_softmax(s, mask):
    s = jnp.where(mask, s.astype(jnp.float32), NEG_INF)
    m = jnp.max(s, axis=-1, keepdims=True)
    e = jnp.where(mask, jnp.exp(s - m), 0.0)
    return e / jnp.maximum(jnp.sum(e, axis=-1, keepdims=True), TINY)


def partial_rope(x, pos):
    half = ROPE_DIM // 2
    inv = ROPE_THETA ** (-jnp.arange(half, dtype=jnp.float32) / half)
    ang = pos.astype(jnp.float32)[:, None] * inv[None, :]
    cos, sin = jnp.cos(ang)[:, None, :], jnp.sin(ang)[:, None, :]
    xr = x[..., :ROPE_DIM].astype(jnp.float32)
    x1, x2 = xr[..., :half], xr[..., half:]
    rot = jnp.concatenate([x1 * cos - x2 * sin, x2 * cos + x1 * sin], axis=-1).astype(x.dtype)
    return jnp.concatenate([rot, x[..., ROPE_DIM:]], axis=-1)


def _softcap(z):
    return M_GATE_CAP * jnp.tanh(z / M_GATE_CAP)


def mlstm_mixer(x, C0, n0, m0, w_in, b_gate, g_norm, w_out):
    B, T, _ = x.shape
    f32 = jnp.float32
    qd, vd = M_HEADS * M_DQK, M_HEADS * M_DV
    q, k, v, o, ig, fg = jnp.split(mm(x, w_in), [qd, 2 * qd, 2 * qd + vd, 2 * qd + 2 * vd, 2 * qd + 2 * vd + M_HEADS], axis=-1)
    q = q.reshape(B, T, M_HEADS, M_DQK).astype(f32)
    k = k.reshape(B, T, M_HEADS, M_DQK).astype(f32) * (M_DQK ** -0.5)
    v = v.reshape(B, T, M_HEADS, M_DV).astype(f32)
    log_i = _softcap((ig + b_gate[0]).astype(f32))
    log_f = jax.nn.log_sigmoid(_softcap((fg + b_gate[1]).astype(f32)))
    L = math.gcd(T, M_CHUNK)
    nc = T // L

    def chunks(a):
        a = a.reshape(B, nc, L, M_HEADS, *a.shape[3:])
        return jnp.moveaxis(a, (1, 3), (0, 2))

    causal = jnp.tril(jnp.ones((L, L), dtype=bool))

    def step(carry, inp):
        C, n, m = carry
        qc, kc, vc, li, lf = inp
        b = jnp.cumsum(lf, axis=-1)
        dlog = jnp.where(causal, b[..., :, None] - b[..., None, :] + li[..., None, :], -jnp.inf)
        m_inter = b + m[..., None]
        m_s = jnp.maximum(m_inter, jnp.max(dlog, axis=-1))
        a = jnp.exp(dlog - m_s[..., None]) * jnp.einsum('bhsd,bhrd->bhsr', qc, kc)
        w_inter = jnp.exp(m_inter - m_s)
        num = w_inter[..., None] * jnp.einsum('bhsd,bhde->bhse', qc, C) + jnp.einsum('bhsr,bhre->bhse', a, vc)
        den = w_inter * jnp.einsum('bhsd,bhd->bhs', qc, n) + jnp.sum(a, axis=-1)
        h = num / jnp.maximum(jnp.abs(den), jnp.exp(-m_s))[..., None]
        b_last = b[..., -1]
        g_r = b_last[..., None] - b + li
        m_new = jnp.maximum(b_last + m, jnp.max(g_r, axis=-1))
        w_r = jnp.exp(g_r - m_new[..., None])
        w_c = jnp.exp(b_last + m - m_new)
        C_new = w_c[..., None, None] * C + jnp.einsum('bhr,bhrd,bhre->bhde', w_r, kc, vc)
        n_new = w_c[..., None] * n + jnp.einsum('bhr,bhrd->bhd', w_r, kc)
        return (C_new, n_new, m_new), h

    (C, n, m), h = lax.scan(step, (C0.astype(f32), n0.astype(f32), m0.astype(f32)),
                            (chunks(q), chunks(k), chunks(v), chunks(log_i), chunks(log_f)))
    h = jnp.moveaxis(h, (0, 2), (1, 3)).reshape(B, T, M_HEADS, M_DV)
    h = h * lax.rsqrt(jnp.mean(h * h, axis=-1, keepdims=True) + NORM_EPS) * g_norm
    h = h * jax.nn.sigmoid(o.reshape(B, T, M_HEADS, M_DV).astype(f32))
    y = mm(h.reshape(B, T, M_HEADS * M_DV), w_out)
    return y, (C.astype(C0.dtype), n.astype(n0.dtype), m.astype(m0.dtype))


def rwkv7_mixer(x, S0, shift0, mix, vec, w_rkv, w1, w2, a1, a2, g1, g2, wo):
    B, T, D = x.shape
    f32 = jnp.float32
    xx = jnp.concatenate([shift0[:, None, :].astype(x.dtype), x[:, :-1]], axis=1) - x

    def lerp(j):
        return x + xx * mix[j]

    w0, a0, k_k, k_a, r_k, gn_w, gn_b = vec[0], vec[1], vec[2], vec[3], vec[4], vec[5], vec[6]
    r = mm(lerp(0), w_rkv[0])
    k = mm(lerp(2), w_rkv[1])
    v = mm(lerp(3), w_rkv[2])
    w_log = -jax.nn.softplus(-(w0 + mm(jnp.tanh(mm(lerp(1), w1)), w2)).astype(f32)) - 0.5
    decay = jnp.exp(-jnp.exp(w_log))
    a = jax.nn.sigmoid((a0 + mm(mm(lerp(4), a1), a2)).astype(f32))
    g = mm(jax.nn.sigmoid(mm(lerp(5), g1)), g2)

    def heads(z):
        return z.reshape(B, T, R_HEADS, R_HEAD).astype(f32)

    kk = heads(k * k_k)
    kk = kk / jnp.maximum(jnp.sqrt(jnp.sum(kk * kk, axis=-1, keepdims=True)), 1e-12)
    k = k.astype(f32) * (1.0 + (a - 1.0) * k_a)
    r_h, k_h, v_h, a_h, w_h = heads(r), heads(k), heads(v), heads(a), heads(decay)

    def step(S, inp):
        r_t, w_t, k_t, v_t, kk_t, a_t = inp
        sa = jnp.einsum('bhvk,bhk->bhv', S, -kk_t)
        S = S * w_t[:, :, None, :] + sa[..., None] * (kk_t * a_t)[:, :, None, :] + v_t[..., None] * k_t[:, :, None, :]
        return S, jnp.einsum('bhvk,bhk->bhv', S, r_t)

    S, y = lax.scan(step, S0.astype(f32),
                    (jnp.moveaxis(r_h, 1, 0), jnp.moveaxis(w_h, 1, 0), jnp.moveaxis(k_h, 1, 0),
                     jnp.moveaxis(v_h, 1, 0), jnp.moveaxis(kk, 1, 0), jnp.moveaxis(a_h, 1, 0)))
    y = jnp.moveaxis(y, 0, 1)
    mu = jnp.mean(y, axis=-1, keepdims=True)
    var = jnp.mean((y - mu) ** 2, axis=-1, keepdims=True)
    y = ((y - mu) * lax.rsqrt(var + R_GN_EPS)).reshape(B, T, D) * gn_w + gn_b
    bonus = jnp.sum(r_h * k_h * r_k.reshape(R_HEADS, R_HEAD), axis=-1, keepdims=True) * v_h
    y = (y + bonus.reshape(B, T, D)) * g
    return mm(y, wo), (S.astype(S0.dtype), x[:, -1])


def shortconv_mixer(x, buf, w_in, conv_w, w_out):
    T = x.shape[1]
    b_gate, c_gate, u = jnp.split(mm(x, w_in), 3, axis=-1)
    z = jnp.concatenate([buf.astype(x.dtype), c_gate * u], axis=1)
    conv = lax.conv_general_dilated(z, conv_w[:, None, :].astype(z.dtype), window_strides=(1,), padding='VALID',
                                    dimension_numbers=('NWC', 'WIO', 'NWC'), feature_group_count=D_MODEL)
    return mm(b_gate * conv, w_out), z[:, T:]


def nsa_project(x, pos, w_in):
    B, T, _ = x.shape
    qd, kvd = A_HEADS * A_HD, 6 * A_KV * A_HD
    q, kv, gate = jnp.split(mm(x, w_in), [qd, qd + kvd], axis=-1)
    q = q.reshape(B, T, A_HEADS, A_HD)
    kv = kv.reshape(B, T, 6, A_KV, A_HD)
    rows = jnp.stack([kv[:, :, 0], kv[:, :, 1], partial_rope(kv[:, :, 2], pos), kv[:, :, 3]], axis=2)
    win = jnp.stack([partial_rope(kv[:, :, 4], pos), kv[:, :, 5]], axis=2)
    gate = jax.nn.sigmoid(gate.reshape(B, T, A_HEADS, 3).astype(jnp.float32))
    return q, partial_rope(q, pos), rows, win, gate


def nsa_compress(kv_cmp, pe, w1, w2):
    B, T = kv_cmp.shape[:2]
    nc = (T - CMP_LEN) // CMP_STRIDE + 1
    idx = (jnp.arange(nc) * CMP_STRIDE)[:, None] + jnp.arange(CMP_LEN)[None, :]
    blk = kv_cmp[:, idx] + jnp.transpose(pe, (1, 0, 2))[:, :, None, :]
    flat = jnp.transpose(blk, (0, 1, 3, 4, 2, 5)).reshape(B, nc, 2, A_KV, CMP_LEN * A_HD)
    hid = jax.nn.gelu(jnp.einsum('bnckf,cfm->bnckm', flat, w1))
    out = jnp.einsum('bnckm,cmd->bnckd', hid, w2)
    return out[:, :, 0], out[:, :, 1]


def nsa_attend(q, q_rot, q_pos, kc, vc, ks, vs, kw, vw, kw_pos, gate):
    B, Qn = q.shape[:2]
    f32 = jnp.float32
    scale = A_HD ** -0.5
    qc = q.reshape(B, Qn, A_KV, A_HPG, A_HD).astype(f32) * scale
    qs = q_rot.reshape(B, Qn, A_KV, A_HPG, A_HD).astype(f32) * scale
    t = q_pos
    NC = kc.shape[1]
    c_start = jnp.arange(NC) * CMP_STRIDE
    c_mask = (c_start + CMP_LEN - 1)[None, :] <= t[:, None]
    p_c = masked_softmax(jnp.einsum('bqghd,bngd->bghqn', qc, kc), c_mask)
    o_c = jnp.einsum('bghqn,bngd->bqghd', p_c, vc)
    TK = ks.shape[1]
    NS = -(-TK // SEL_BLK)

    def blocks(a):
        a = jnp.pad(a, ((0, 0), (0, NS * SEL_BLK - TK), (0, 0), (0, 0)))
        return jnp.transpose(a.reshape(B, NS, SEL_BLK, A_KV, A_HD), (0, 3, 1, 2, 4))

    ks_b, vs_b = blocks(ks), blocks(vs)
    s_start = jnp.arange(NS) * SEL_BLK
    overlap = ((c_start[:, None] < s_start[None, :] + SEL_BLK) &
               (c_start[:, None] + CMP_LEN > s_start[None, :])).astype(f32)
    imp = jnp.einsum('bghqn,ns->bgqs', p_c, overlap)
    cur = (t // SEL_BLK)[:, None]
    j = jnp.arange(NS)[None, :]
    valid = s_start[None, :] <= t[:, None]
    forced = (j == 0) | (j == cur) | (j == cur - 1)
    score = jnp.where(valid, jnp.where(forced, FORCE_SCORE, imp), NEG_INF)
    n_top = min(SEL_TOP, NS)
    _, idx = lax.top_k(score, n_top)
    sel_ok = jnp.take_along_axis(jnp.broadcast_to(valid, score.shape), idx, axis=-1)
    bi = jnp.arange(B)[:, None, None, None]
    gi = jnp.arange(A_KV)[None, :, None, None]
    k_sel = ks_b[bi, gi, idx]
    v_sel = vs_b[bi, gi, idx].reshape(B, A_KV, Qn, n_top * SEL_BLK, A_HD)
    key_pos = idx[..., None] * SEL_BLK + jnp.arange(SEL_BLK)
    s_mask = (key_pos <= t[None, None, :, None, None]) & sel_ok[..., None]
    s_s = jnp.einsum('bqghd,bgqnkd->bghqnk', qs, k_sel).reshape(B, A_KV, A_HPG, Qn, n_top * SEL_BLK)
    p_s = masked_softmax(s_s, s_mask.reshape(B, A_KV, 1, Qn, n_top * SEL_BLK))
    o_s = jnp.einsum('bghqm,bgqmd->bqghd', p_s, v_sel)
    w_mask = ((kw_pos[None, :] <= t[:, None]) & (kw_pos[None, :] > t[:, None] - WINDOW) & (kw_pos[None, :] >= 0))
    p_w = masked_softmax(jnp.einsum('bqghd,bkgd->bghqk', qs, kw), w_mask)
    o_w = jnp.einsum('bghqk,bkgd->bqghd', p_w, vw)
    g = gate.reshape(B, Qn, A_KV, A_HPG, 3)
    o = g[..., 0:1] * o_c + g[..., 1:2] * o_s + g[..., 2:3] * o_w
    return o.reshape(B, Qn, A_HEADS * A_HD)


def nsa_prompt(x, w_in, pe, w1, w2, w_out):
    B, T, _ = x.shape
    pos = jnp.arange(T, dtype=jnp.int32)
    q, q_rot, rows, win, gate = nsa_project(x, pos, w_in)
    kc, vc = nsa_compress(rows[:, :, 0:2], pe, w1, w2)
    ks, vs = rows[:, :, 2], rows[:, :, 3]
    win_pad = jnp.pad(win, ((0, 0), (WINDOW, 0), (0, 0), (0, 0), (0, 0)))
    QB = math.gcd(T, NSA_QBLK)

    def block(b):
        s = b * QB
        wb = lax.dynamic_slice_in_dim(win_pad, s, WINDOW + QB, axis=1)
        return nsa_attend(lax.dynamic_slice_in_dim(q, s, QB, axis=1),
                          lax.dynamic_slice_in_dim(q_rot, s, QB, axis=1),
                          s + jnp.arange(QB, dtype=jnp.int32), kc, vc, ks, vs, wb[:, :, 0], wb[:, :, 1],
                          s - WINDOW + jnp.arange(WINDOW + QB, dtype=jnp.int32),
                          lax.dynamic_slice_in_dim(gate, s, QB, axis=1))

    o = lax.map(block, jnp.arange(T // QB, dtype=jnp.int32))
    o = jnp.transpose(o, (1, 0, 2, 3)).reshape(B, T, A_HEADS * A_HD)
    return mm(o, w_out), rows, win[:, T - min(WINDOW, T):]


def nsa_sample(x, cache_kv, page_table, win_buf, w_in, pe, w1, w2, w_out):
    B, S, _ = x.shape
    past = page_table.shape[1] * PAGE_SIZE
    pos = past + jnp.arange(S, dtype=jnp.int32)
    q, q_rot, rows, win, gate = nsa_project(x, pos, w_in)
    past_rows = cache_kv[page_table].reshape(B, past, 4, A_KV, A_HD)
    all_rows = jnp.concatenate([past_rows.astype(rows.dtype), rows], axis=1)
    kc, vc = nsa_compress(all_rows[:, :, 0:2], pe, w1, w2)
    wb_len = win_buf.shape[1]
    kw_all = jnp.concatenate([win_buf.astype(win.dtype), win], axis=1)
    kw_pos = past - wb_len + jnp.arange(wb_len + S, dtype=jnp.int32)
    o = nsa_attend(q, q_rot, pos, kc, vc, all_rows[:, :, 2], all_rows[:, :, 3],
                   kw_all[:, :, 0], kw_all[:, :, 1], kw_pos, gate)
    return mm(o, w_out), rows, kw_all[:, S:]


def kernel(x_prompt, x_sample, state_mlstm_C, state_mlstm_n, state_mlstm_m, state_rwkv_S, state_rwkv_shift,
           state_conv, cache_nsa_kv, state_nsa_win, page_table, norm_g, mlp_up, mlp_down,
           m_w_in, m_b_gate, m_norm, m_w_out, r_mix, r_vec, r_w_rkv, r_w1, r_w2, r_a1, r_a2, r_g1, r_g2, r_wo,
           c_w_in, c_conv, c_w_out, a_w_in, a_cmp_pe, a_cmp_w1, a_cmp_w2, a_w_out):
    xp, xs = x_prompt, x_sample
    Bp = xp.shape[0]
    dt = xp.dtype
    depth = norm_g.shape[0]
    for i in range(depth):
        kind = i % N_MIXERS
        hp, hs = rms_norm(xp, norm_g[i, 0]), rms_norm(xs, norm_g[i, 0])
        if kind == 0:
            yp, (mC_p, mn_p, mm_p) = mlstm_mixer(hp, jnp.zeros((Bp, M_HEADS, M_DQK, M_DV), dt),
                                                 jnp.zeros((Bp, M_HEADS, M_DQK), dt), jnp.zeros((Bp, M_HEADS), dt),
                                                 m_w_in, m_b_gate, m_norm, m_w_out)
            ys, (mC_s, mn_s, mm_s) = mlstm_mixer(hs, state_mlstm_C, state_mlstm_n, state_mlstm_m,
                                                 m_w_in, m_b_gate, m_norm, m_w_out)
        elif kind == 1:
            yp, (rS_p, rx_p) = rwkv7_mixer(hp, jnp.zeros((Bp, R_HEADS, R_HEAD, R_HEAD), dt), jnp.zeros((Bp, D_MODEL), dt),
                                           r_mix, r_vec, r_w_rkv, r_w1, r_w2, r_a1, r_a2, r_g1, r_g2, r_wo)
            ys, (rS_s, rx_s) = rwkv7_mixer(hs, state_rwkv_S, state_rwkv_shift,
                                           r_mix, r_vec, r_w_rkv, r_w1, r_w2, r_a1, r_a2, r_g1, r_g2, r_wo)
        elif kind == 2:
            yp, cv_p = shortconv_mixer(hp, jnp.zeros((Bp, CONV_W - 1, D_MODEL), dt), c_w_in, c_conv, c_w_out)
            ys, cv_s = shortconv_mixer(hs, state_conv, c_w_in, c_conv, c_w_out)
        else:
            yp, kv_p, win_p = nsa_prompt(hp, a_w_in, a_cmp_pe, a_cmp_w1, a_cmp_w2, a_w_out)
            ys, kv_s, win_s = nsa_sample(hs, cache_nsa_kv, page_table, state_nsa_win,
                                         a_w_in, a_cmp_pe, a_cmp_w1, a_cmp_w2, a_w_out)
        xp = xp + rms_norm(yp, norm_g[i, 1])
        xs = xs + rms_norm(ys, norm_g[i, 1])
        xp = xp + rms_norm(sq_relu_mlp(rms_norm(xp, norm_g[i, 2]), mlp_up[i], mlp_down[i]), norm_g[i, 3])
        xs = xs + rms_norm(sq_relu_mlp(rms_norm(xs, norm_g[i, 2]), mlp_up[i], mlp_down[i]), norm_g[i, 3])
    return (xp, xs, mC_p, mC_s, mn_p, mn_s, mm_p, mm_s, rS_p, rS_s, rx_p, rx_s, cv_p, cv_s, kv_p, kv_s, win_p, win_s)
```

```python
import functools
import math

import jax
import jax.numpy as jnp
from jax import lax
from jax.experimental import pallas as pl
from jax.experimental.pallas import tpu as pltpu

D_MODEL = 4096
N_MIXERS = 4
NORM_EPS = 1e-6
NEG_INF = -1e30
TINY = 1e-30

M_HEADS = 8
M_DV = D_MODEL // M_HEADS
M_DQK = M_DV // 2
M_CHUNK = 64
M_GATE_CAP = 15.0

R_HEAD = 64
R_HEADS = D_MODEL // R_HEAD
R_GN_EPS = 64e-5

CONV_W = 3

A_HEADS = 32
A_HD = D_MODEL // A_HEADS
A_KV = 4
A_HPG = A_HEADS // A_KV
ROPE_DIM = A_HD // 4
ROPE_THETA = 500000.0
CMP_LEN = 32
CMP_STRIDE = 16
SEL_BLK = 64
SEL_TOP = 16
WINDOW = 512
NSA_QBLK = 64
FORCE_SCORE = 1e9
PAGE_SIZE = 128

VMEM_LIMIT_BYTES = 52 * 1024 * 1024


def _epilogue(acc, epilogue):
    if epilogue == 'relu2':
        r = jnp.maximum(acc, 0.0)
        return r * r
    return acc


def _mm_fullk_kernel(x_ref, w_ref, o_ref, *, epilogue):
    acc = jnp.dot(x_ref[...], w_ref[...].astype(jnp.bfloat16), preferred_element_type=jnp.float32)
    o_ref[...] = _epilogue(acc, epilogue).astype(o_ref.dtype)


def _mm_splitk_kernel(x_ref, w_ref, o_ref):
    @pl.when(pl.program_id(2) == 0)
    def _():
        o_ref[...] = jnp.zeros_like(o_ref)

    o_ref[...] += jnp.dot(x_ref[...], w_ref[...].astype(jnp.bfloat16), preferred_element_type=jnp.float32)


def _pick(n, pref):
    for t in pref:
        if n % t == 0:
            return t
    return n


MM_FULLK_MAX = 4096


def mm(x, w, epilogue=None, out_dtype=jnp.float32):
    lead = x.shape[:-1]
    K = x.shape[-1]
    N = w.shape[1]
    x2 = x.reshape(-1, K).astype(jnp.bfloat16)
    M = x2.shape[0]
    tm = _pick(M, (2048, 1024, 512, 256, 128, 64, 32, 16))
    if K <= MM_FULLK_MAX:
        tn = 256 if N > 512 else N
        out = pl.pallas_call(
            functools.partial(_mm_fullk_kernel, epilogue=epilogue),
            grid=(M // tm, pl.cdiv(N, tn)),
            in_specs=[pl.BlockSpec((tm, K), lambda i, j: (i, 0), pipeline_mode=pl.Buffered(1)),
                      pl.BlockSpec((K, tn), lambda i, j: (0, j))],
            out_specs=pl.BlockSpec((tm, tn), lambda i, j: (i, j)),
            out_shape=jax.ShapeDtypeStruct((M, N), out_dtype),
            compiler_params=pltpu.CompilerParams(
                dimension_semantics=("parallel", "parallel"),
                vmem_limit_bytes=VMEM_LIMIT_BYTES),
            name="mm_fullk",
        )(x2, w)
    else:
        assert epilogue is None and out_dtype == jnp.float32
        tn = _pick(N, (1024, 512, 256, 128))
        tk = _pick(K, (1024, 512, 256, 128))
        out = pl.pallas_call(
            _mm_splitk_kernel,
            grid=(M // tm, N // tn, K // tk),
            in_specs=[pl.BlockSpec((tm, tk), lambda i, j, k: (i, k)),
                      pl.BlockSpec((tk, tn), lambda i, j, k: (k, j))],
            out_specs=pl.BlockSpec((tm, tn), lambda i, j, k: (i, j)),
            out_shape=jax.ShapeDtypeStruct((M, N), jnp.float32),
            compiler_params=pltpu.CompilerParams(
                dimension_semantics=("parallel", "parallel", "arbitrary"),
                vmem_limit_bytes=VMEM_LIMIT_BYTES),
            name="mm_splitk",
        )(x2, w)
    return out.reshape(*lead, N)


def rms_norm(x, g):
    xf = x.astype(jnp.float32)
    y = xf * lax.rsqrt(jnp.mean(xf * xf, axis=-1, keepdims=True) + NORM_EPS)
    return (y * g.astype(jnp.float32)).astype(x.dtype)


def sq_relu_mlp(x, w_up, w_down):
    h = mm(x, w_up, epilogue='relu2', out_dtype=jnp.bfloat16)
    return mm(h, w_down)


def masked_softmax(s, mask):
    s = jnp.where(mask, s.astype(jnp.float32), NEG_INF)
    m = jnp.max(s, axis=-1, keepdims=True)
    e = jnp.where(mask, jnp.exp(s - m), 0.0)
    return e / jnp.maximum(jnp.sum(e, axis=-1, keepdims=True), TINY)


def partial_rope(x, pos):
    half = ROPE_DIM // 2
    inv = ROPE_THETA ** (-jnp.arange(half, dtype=jnp.float32) / half)
    ang = pos.astype(jnp.float32)[:, None] * inv[None, :]
    cos, sin = jnp.cos(ang)[:, None, :], jnp.sin(ang)[:, None, :]
    xr = x[..., :ROPE_DIM].astype(jnp.float32)
    x1, x2 = xr[..., :half], xr[..., half:]
    rot = jnp.concatenate([x1 * cos - x2 * sin, x2 * cos + x1 * sin], axis=-1).astype(x.dtype)
    return jnp.concatenate([rot, x[..., ROPE_DIM:]], axis=-1)


def _softcap(z):
    return M_GATE_CAP * jnp.tanh(z / M_GATE_CAP)


def mlstm_mixer(x, C0, n0, m0, w_in, b_gate, g_norm, w_out):
    B, T, _ = x.shape
    f32 = jnp.float32
    qd, vd = M_HEADS * M_DQK, M_HEADS * M_DV
    q, k, v, o, ig, fg = jnp.split(mm(x, w_in), [qd, 2 * qd, 2 * qd + vd, 2 * qd + 2 * vd, 2 * qd + 2 * vd + M_HEADS], axis=-1)
    q = q.reshape(B, T, M_HEADS, M_DQK).astype(f32)
    k = k.reshape(B, T, M_HEADS, M_DQK).astype(f32) * (M_DQK ** -0.5)
    v = v.reshape(B, T, M_HEADS, M_DV).astype(f32)
    log_i = _softcap((ig + b_gate[0]).astype(f32))
    log_f = jax.nn.log_sigmoid(_softcap((fg + b_gate[1]).astype(f32)))
    L = math.gcd(T, M_CHUNK)
    nc = T // L

    def chunks(a):
        a = a.reshape(B, nc, L, M_HEADS, *a.shape[3:])
        return jnp.moveaxis(a, (1, 3), (0, 2))

    causal = jnp.tril(jnp.ones((L, L), dtype=bool))

    def step(carry, inp):
        C, n, m = carry
        qc, kc, vc, li, lf = inp
        b = jnp.cumsum(lf, axis=-1)
        dlog = jnp.where(causal, b[..., :, None] - b[..., None, :] + li[..., None, :], -jnp.inf)
        m_inter = b + m[..., None]
        m_s = jnp.maximum(m_inter, jnp.max(dlog, axis=-1))
        a = jnp.exp(dlog - m_s[..., None]) * jnp.einsum('bhsd,bhrd->bhsr', qc, kc)
        w_inter = jnp.exp(m_inter - m_s)
        num = w_inter[..., None] * jnp.einsum('bhsd,bhde->bhse', qc, C) + jnp.einsum('bhsr,bhre->bhse', a, vc)
        den = w_inter * jnp.einsum('bhsd,bhd->bhs', qc, n) + jnp.sum(a, axis=-1)
        h = num / jnp.maximum(jnp.abs(den), jnp.exp(-m_s))[..., None]
        b_last = b[..., -1]
        g_r = b_last[..., None] - b + li
        m_new = jnp.maximum(b_last + m, jnp.max(g_r, axis=-1))
        w_r = jnp.exp(g_r - m_new[..., None])
        w_c = jnp.exp(b_last + m - m_new)
        C_new = w_c[..., None, None] * C + jnp.einsum('bhr,bhrd,bhre->bhde', w_r, kc, vc)
        n_new = w_c[..., None] * n + jnp.einsum('bhr,bhrd->bhd', w_r, kc)
        return (C_new, n_new, m_new), h

    (C, n, m), h = lax.scan(step, (C0.astype(f32), n0.astype(f32), m0.astype(f32)),
                            (chunks(q), chunks(k), chunks(v), chunks(log_i), chunks(log_f)))
    h = jnp.moveaxis(h, (0, 2), (1, 3)).reshape(B, T, M_HEADS, M_DV)
    h = h * lax.rsqrt(jnp.mean(h * h, axis=-1, keepdims=True) + NORM_EPS) * g_norm
    h = h * jax.nn.sigmoid(o.reshape(B, T, M_HEADS, M_DV).astype(f32))
    y = mm(h.reshape(B, T, M_HEADS * M_DV), w_out)
    return y, (C.astype(C0.dtype), n.astype(n0.dtype), m.astype(m0.dtype))


RWKV_TC = 32
RWKV_ROWS = 4
RWKV_LANES = 128


def _rwkv_scan_kernel(r_ref, k_ref, v_ref, wp_ref, ap_ref, par_ref, s0_ref, y_ref, sout_ref,
                      s_scr, w_scr, b_scr, kx_scr, nkk_scr, y_scr, *, tc, nchunks, rows):
    c = pl.program_id(1)

    @pl.when(c == 0)
    def _():
        s_scr[...] = s0_ref[0]

    def prep(t, carry):
        kt = k_ref[0, t]
        z = -wp_ref[0, t]
        softplus = jnp.maximum(z, 0.0) + jnp.log(1.0 + jnp.exp(-jnp.abs(z)))
        w_scr[t] = jnp.exp(-jnp.exp(-softplus - 0.5))
        a = 1.0 / (1.0 + jnp.exp(-ap_ref[0, t]))
        kk = kt * par_ref[0]
        kk = kk / jnp.maximum(jnp.sqrt(jnp.sum(kk * kk, axis=0, keepdims=True)), 1e-12)
        b_scr[t] = kk * a
        nkk_scr[t] = -kk
        kx_scr[t] = kt * (1.0 + (a - 1.0) * par_ref[1])
        return carry

    lax.fori_loop(0, tc, prep, 0)

    def vblock(vb, carry):
        v0 = vb * rows

        def step(t, S):
            out = []
            for i in range(rows):
                sa = jnp.sum(S[i] * nkk_scr[t], axis=0, keepdims=True)
                vrow = v_ref[0, t, pl.ds(v0 + i, 1), :]
                Si = S[i] * w_scr[t] + sa * b_scr[t] + vrow * kx_scr[t]
                y_scr[t, pl.ds(v0 + i, 1), :] = jnp.sum(Si * r_ref[0, t], axis=0, keepdims=True)
                out.append(Si)
            return tuple(out)

        S = lax.fori_loop(0, tc, step, tuple(s_scr[v0 + i] for i in range(rows)))
        for i in range(rows):
            s_scr[v0 + i] = S[i]
        return carry

    lax.fori_loop(0, R_HEAD // rows, vblock, 0)

    def post(t, carry):
        y = y_scr[t]
        mu = jnp.mean(y, axis=0, keepdims=True)
        var = jnp.mean((y - mu) ** 2, axis=0, keepdims=True)
        yn = ((y - mu) * lax.rsqrt(var + R_GN_EPS)) * par_ref[3] + par_ref[4]
        bonus = jnp.sum(r_ref[0, t] * kx_scr[t] * par_ref[2], axis=0, keepdims=True) * v_ref[0, t]
        y_ref[0, t] = yn + bonus
        return carry

    lax.fori_loop(0, tc, post, 0)

    @pl.when(c == nchunks - 1)
    def _():
        sout_ref[0] = s_scr[...]


def _to_chain(z, B, T):
    z = z.reshape(B // 2, 2, T, R_HEADS, R_HEAD)
    return jnp.transpose(z, (0, 2, 4, 1, 3)).reshape(B // 2, T, R_HEAD, RWKV_LANES)


def _vec_to_chain(vec):
    return jnp.tile(vec.reshape(R_HEADS, R_HEAD).T, (1, 2))


def rwkv_scan(r, k, v, w_pre, a_pre, S0, k_k, k_a, r_k, gn_w, gn_b):
    B, T, D = r.shape
    assert B % 2 == 0 and 2 * R_HEADS == RWKV_LANES and T % RWKV_TC == 0
    G = B // 2
    nchunks = T // RWKV_TC
    f32 = jnp.float32
    ins = [_to_chain(z.astype(f32), B, T) for z in (r, k, v, w_pre, a_pre)]
    par = jnp.stack([_vec_to_chain(p.astype(f32)) for p in (k_k, k_a, r_k, gn_w, gn_b)])
    s0 = jnp.transpose(S0.astype(f32).reshape(G, 2, R_HEADS, R_HEAD, R_HEAD), (0, 3, 4, 1, 2))
    s0 = s0.reshape(G, R_HEAD, R_HEAD, RWKV_LANES)
    seq_spec = pl.BlockSpec((1, RWKV_TC, R_HEAD, RWKV_LANES), lambda g, c: (g, c, 0, 0))
    st_spec = pl.BlockSpec((1, R_HEAD, R_HEAD, RWKV_LANES), lambda g, c: (g, 0, 0, 0))
    seq_scr = pltpu.VMEM((RWKV_TC, R_HEAD, RWKV_LANES), f32)
    y, s_out = pl.pallas_call(
        functools.partial(_rwkv_scan_kernel, tc=RWKV_TC, nchunks=nchunks, rows=RWKV_ROWS),
        grid=(G, nchunks),
        in_specs=[seq_spec] * 5 + [pl.BlockSpec((5, R_HEAD, RWKV_LANES), lambda g, c: (0, 0, 0)), st_spec],
        out_specs=[seq_spec, st_spec],
        out_shape=[jax.ShapeDtypeStruct((G, T, R_HEAD, RWKV_LANES), f32),
                   jax.ShapeDtypeStruct((G, R_HEAD, R_HEAD, RWKV_LANES), f32)],
        scratch_shapes=[pltpu.VMEM((R_HEAD, R_HEAD, RWKV_LANES), f32)] + [seq_scr] * 5,
        compiler_params=pltpu.CompilerParams(
            dimension_semantics=("parallel", "arbitrary"),
            vmem_limit_bytes=VMEM_LIMIT_BYTES),
        name="rwkv_scan",
    )(*ins, par, s0)
    y = jnp.transpose(y.reshape(G, T, R_HEAD, 2, R_HEADS), (0, 3, 1, 4, 2)).reshape(B, T, D)
    s_out = jnp.transpose(s_out.reshape(G, R_HEAD, R_HEAD, 2, R_HEADS), (0, 3, 4, 1, 2))
    return y, s_out.reshape(B, R_HEADS, R_HEAD, R_HEAD)


def rwkv7_mixer(x, S0, shift0, mix, vec, w_rkv, w1, w2, a1, a2, g1, g2, wo, use_scan_kernel):
    B, T, D = x.shape
    f32 = jnp.float32
    bf16 = jnp.bfloat16
    xx = jnp.concatenate([shift0[:, None, :].astype(x.dtype), x[:, :-1]], axis=1) - x

    def lerp(j):
        return (x + xx * mix[j]).astype(bf16)

    w0, a0, k_k, k_a, r_k, gn_w, gn_b = vec[0], vec[1], vec[2], vec[3], vec[4], vec[5], vec[6]
    r = mm(lerp(0), w_rkv[0])
    k = mm(lerp(2), w_rkv[1])
    v = mm(lerp(3), w_rkv[2])
    w_pre = (w0 + mm(jnp.tanh(mm(lerp(1), w1)).astype(bf16), w2)).astype(f32)
    a_pre = (a0 + mm(mm(lerp(4), a1).astype(bf16), a2)).astype(f32)
    g = mm(jax.nn.sigmoid(mm(lerp(5), g1)).astype(bf16), g2)
    if use_scan_kernel:
        y, S = rwkv_scan(r, k, v, w_pre, a_pre, S0, k_k, k_a, r_k, gn_w, gn_b)
        return mm((y * g).astype(bf16), wo), (S.astype(S0.dtype), x[:, -1])

    w_log = -jax.nn.softplus(-w_pre) - 0.5
    decay = jnp.exp(-jnp.exp(w_log))
    a = jax.nn.sigmoid(a_pre)

    def heads(z):
        return z.reshape(B, T, R_HEADS, R_HEAD).astype(f32)

    kk = heads(k * k_k)
    kk = kk / jnp.maximum(jnp.sqrt(jnp.sum(kk * kk, axis=-1, keepdims=True)), 1e-12)
    k = k.astype(f32) * (1.0 + (a - 1.0) * k_a)
    r_h, k_h, v_h, a_h, w_h = heads(r), heads(k), heads(v), heads(a), heads(decay)

    def step(S, inp):
        r_t, w_t, k_t, v_t, kk_t, a_t = inp
        sa = jnp.sum(S * (-kk_t)[:, :, None, :], axis=-1)
        S = S * w_t[:, :, None, :] + sa[..., None] * (kk_t * a_t)[:, :, None, :] + v_t[..., None] * k_t[:, :, None, :]
        return S, jnp.sum(S * r_t[:, :, None, :], axis=-1)

    S, y = lax.scan(step, S0.astype(f32),
                    (jnp.moveaxis(r_h, 1, 0), jnp.moveaxis(w_h, 1, 0), jnp.moveaxis(k_h, 1, 0),
                     jnp.moveaxis(v_h, 1, 0), jnp.moveaxis(kk, 1, 0), jnp.moveaxis(a_h, 1, 0)))
    y = jnp.moveaxis(y, 0, 1)
    mu = jnp.mean(y, axis=-1, keepdims=True)
    var = jnp.mean((y - mu) ** 2, axis=-1, keepdims=True)
    y = ((y - mu) * lax.rsqrt(var + R_GN_EPS)).reshape(B, T, D) * gn_w + gn_b
    bonus = jnp.sum(r_h * k_h * r_k.reshape(R_HEADS, R_HEAD), axis=-1, keepdims=True) * v_h
    y = (y + bonus.reshape(B, T, D)) * g
    return mm(y.astype(bf16), wo), (S.astype(S0.dtype), x[:, -1])


def shortconv_mixer(x, buf, w_in, conv_w, w_out):
    T = x.shape[1]
    b_gate, c_gate, u = jnp.split(mm(x, w_in), 3, axis=-1)
    z = jnp.concatenate([buf.astype(x.dtype), c_gate * u], axis=1)
    conv = sum(z[:, j:j + T] * conv_w[j] for j in range(CONV_W))
    return mm(b_gate * conv, w_out), z[:, T:]


NSA_QB = 128
NSA_KT = 512
NSA_WT = WINDOW + NSA_QB
NSA_SLOTS = 128


def _dot_nt(a, b):
    return lax.dot_general(a, b, (((1,), (1,)), ((), ())), preferred_element_type=jnp.float32)


def _softmax_rows(s, mask):
    s = jnp.where(mask, s, NEG_INF)
    m = jnp.max(s, axis=-1, keepdims=True)
    e = jnp.where(mask, jnp.exp(s - m), 0.0)
    return e / jnp.maximum(jnp.sum(e, axis=-1, keepdims=True), TINY)


def _nsa_attn_kernel(q_ref, ks_ref, vs_ref, kw_ref, vw_ref, kc_ref, vc_ref, gate_ref, cos_ref, sna_ref, snb_ref,
                     ovl_ref, exp_ref, o_ref, *, qb, seq, n_cmp, n_sel):
    f32, bf16 = jnp.float32, jnp.bfloat16
    t0 = pl.program_id(2) * qb
    scale = A_HD ** -0.5
    cos, sna, snb = cos_ref[...], sna_ref[...], snb_ref[...]

    def rope(xh):
        return xh * cos + pltpu.roll(xh, A_HD - ROPE_DIM // 2, 1) * sna + pltpu.roll(xh, ROPE_DIM // 2, 1) * snb

    heads = [q_ref[0, :, h * A_HD:(h + 1) * A_HD] for h in range(A_HPG)]
    qc = jnp.concatenate([xh * scale for xh in heads], axis=0)
    qs = jnp.concatenate([(rope(xh) * scale).astype(bf16) for xh in heads], axis=0)
    rows = A_HPG * qb
    tq = t0 + lax.broadcasted_iota(jnp.int32, (qb, 1), 0)
    tq3 = tq[None]

    lane = lax.broadcasted_iota(jnp.int32, (1, 1, NSA_SLOTS), 2)
    kc = kc_ref[0]
    qc_hi, kc_hi = qc.astype(bf16), kc.astype(bf16)
    qc_lo, kc_lo = (qc - qc_hi.astype(f32)).astype(bf16), (kc - kc_hi.astype(f32)).astype(bf16)
    s_c = (_dot_nt(qc_hi, kc_hi) + _dot_nt(qc_hi, kc_lo) + _dot_nt(qc_lo, kc_hi)).reshape(A_HPG, qb, NSA_SLOTS)
    c_mask = (lane * CMP_STRIDE + (CMP_LEN - 1) <= tq3) & (lane < n_cmp)
    p_c = _softmax_rows(s_c, c_mask)
    o_c = jnp.dot(p_c.reshape(rows, NSA_SLOTS).astype(bf16), vc_ref[0].astype(bf16), preferred_element_type=f32)
    psum = jnp.sum(p_c, axis=0)
    ovl = ovl_ref[...]
    hi = psum.astype(bf16)
    r1 = psum - hi.astype(f32)
    mid = r1.astype(bf16)
    lo = (r1 - mid.astype(f32)).astype(bf16)
    imp = (jnp.dot(hi, ovl, preferred_element_type=f32) + jnp.dot(mid, ovl, preferred_element_type=f32)
           + jnp.dot(lo, ovl, preferred_element_type=f32))

    sidx = lax.broadcasted_iota(jnp.int32, (qb, NSA_SLOTS), 1)
    cur = lax.shift_right_logical(tq, SEL_BLK.bit_length() - 1)
    valid = (sidx * SEL_BLK <= tq) & (sidx < n_sel)
    forced = (sidx == 0) | (sidx == cur) | (sidx == cur - 1)
    score = jnp.where(valid, jnp.where(forced, FORCE_SCORE, imp), NEG_INF)
    rank = jnp.zeros((qb, NSA_SLOTS), jnp.int32)
    for j in range(n_sel):
        cj = score[:, j:j + 1]
        beats = (cj > score) | ((cj == score) & (sidx > j))
        rank = rank + beats.astype(jnp.int32)
    sel = ((rank < SEL_TOP) & valid).astype(bf16)

    kpos0 = lax.broadcasted_iota(jnp.int32, (qb, NSA_KT), 1)

    def sel_tile(kt, carry):
        m, l, acc = carry
        k0 = pl.multiple_of(kt * NSA_KT, NSA_KT)
        s = _dot_nt(qs, ks_ref[0, pl.ds(k0, NSA_KT), :].astype(bf16)).reshape(A_HPG, qb, NSA_KT)
        picked = jnp.dot(sel, exp_ref[kt], preferred_element_type=f32) > 0.5
        mask = (picked & (kpos0 + k0 <= tq))[None]
        s = jnp.where(mask, s, NEG_INF)
        m_new = jnp.maximum(m, jnp.max(s, axis=-1, keepdims=True))
        alpha = jnp.exp(m - m_new)
        e = jnp.where(mask, jnp.exp(s - m_new), 0.0)
        l = alpha * l + jnp.sum(e, axis=-1, keepdims=True)
        pv = jnp.dot(e.reshape(rows, NSA_KT).astype(bf16), vs_ref[0, pl.ds(k0, NSA_KT), :].astype(bf16),
                     preferred_element_type=f32)
        acc = alpha * acc + pv.reshape(A_HPG, qb, A_HD)
        return m_new, l, acc

    n_tiles = (t0 + qb + NSA_KT - 1) // NSA_KT
    m0 = jnp.full((A_HPG, qb, 1), NEG_INF, f32)
    _, l_s, acc_s = lax.fori_loop(0, n_tiles, sel_tile,
                                  (m0, jnp.zeros((A_HPG, qb, 1), f32), jnp.zeros((A_HPG, qb, A_HD), f32)))
    o_s = acc_s / jnp.maximum(l_s, TINY)

    w0 = pl.multiple_of(jnp.clip(t0 + qb - NSA_WT, 0, seq - NSA_WT), qb)
    s_w = _dot_nt(qs, kw_ref[0, pl.ds(w0, NSA_WT), :].astype(bf16)).reshape(A_HPG, qb, NSA_WT)
    wpos = w0 + lax.broadcasted_iota(jnp.int32, (1, 1, NSA_WT), 2)
    w_mask = (wpos <= tq3) & (wpos > tq3 - WINDOW)
    p_w = _softmax_rows(s_w, w_mask)
    o_w = jnp.dot(p_w.reshape(rows, NSA_WT).astype(bf16), vw_ref[0, pl.ds(w0, NSA_WT), :].astype(bf16),
                  preferred_element_type=f32).reshape(A_HPG, qb, A_HD)

    o_c = o_c.reshape(A_HPG, qb, A_HD)
    gate = 1.0 / (1.0 + jnp.exp(-gate_ref[0, 0]))
    for h in range(A_HPG):
        o_ref[0, :, h * A_HD:(h + 1) * A_HD] = (gate[:, 3 * h:3 * h + 1] * o_c[h]
                                                + gate[:, 3 * h + 1:3 * h + 2] * o_s[h]
                                                + gate[:, 3 * h + 2:3 * h + 3] * o_w[h])


def nsa_attend_prompt(q, rows, win, kc, vc, gate_logits):
    B, T, _ = q.shape
    f32, bf16 = jnp.float32, jnp.bfloat16
    assert T % NSA_QB == 0 and T % NSA_KT == 0 and T >= NSA_WT and T % SEL_BLK == 0
    n_cmp = kc.shape[1]
    n_sel = T // SEL_BLK
    assert n_cmp <= NSA_SLOTS and n_sel <= NSA_SLOTS
    pad = ((0, 0), (0, NSA_SLOTS - n_cmp), (0, 0), (0, 0))
    kc_p = jnp.pad(kc, pad).reshape(B, NSA_SLOTS, A_KV * A_HD)
    vc_p = jnp.pad(vc, pad).reshape(B, NSA_SLOTS, A_KV * A_HD)
    gate_g = jnp.transpose(gate_logits.reshape(B, T, A_KV, 3 * A_HPG), (0, 2, 1, 3)).astype(f32)
    half = ROPE_DIM // 2
    inv = ROPE_THETA ** (-jnp.arange(half, dtype=f32) / half)
    ang = jnp.arange(T, dtype=jnp.int32).astype(f32)[:, None] * inv[None, :]
    zeros = jnp.zeros((T, A_HD - ROPE_DIM), f32)
    cos_t = jnp.concatenate([jnp.cos(ang), jnp.cos(ang), 1.0 + zeros], axis=1)
    sna_t = jnp.concatenate([-jnp.sin(ang), jnp.zeros((T, half), f32), zeros], axis=1)
    snb_t = jnp.concatenate([jnp.zeros((T, half), f32), jnp.sin(ang), zeros], axis=1)
    c_start = jnp.arange(NSA_SLOTS) * CMP_STRIDE
    s_start = jnp.arange(NSA_SLOTS) * SEL_BLK
    ovl = ((c_start[:, None] < s_start[None, :] + SEL_BLK) & (c_start[:, None] + CMP_LEN > s_start[None, :])
           & (jnp.arange(NSA_SLOTS)[:, None] < n_cmp) & (jnp.arange(NSA_SLOTS)[None, :] < n_sel)).astype(bf16)
    expand = (jnp.arange(NSA_SLOTS)[:, None] == (jnp.arange(T)[None, :] // SEL_BLK)).astype(bf16)
    expand = jnp.transpose(expand.reshape(NSA_SLOTS, T // NSA_KT, NSA_KT), (1, 0, 2))

    def kv_spec(col0):
        return pl.BlockSpec((1, T, A_HD), lambda b, g, i: (b, 0, col0 + g))

    tab_spec = pl.BlockSpec((NSA_QB, A_HD), lambda b, g, i: (i, 0))
    return pl.pallas_call(
        functools.partial(_nsa_attn_kernel, qb=NSA_QB, seq=T, n_cmp=n_cmp, n_sel=n_sel),
        grid=(B, A_KV, T // NSA_QB),
        in_specs=[pl.BlockSpec((1, NSA_QB, A_HPG * A_HD), lambda b, g, i: (b, i, g)),
                  kv_spec(2 * A_KV), kv_spec(3 * A_KV), kv_spec(0), kv_spec(A_KV),
                  pl.BlockSpec((1, NSA_SLOTS, A_HD), lambda b, g, i: (b, 0, g)),
                  pl.BlockSpec((1, NSA_SLOTS, A_HD), lambda b, g, i: (b, 0, g)),
                  pl.BlockSpec((1, 1, NSA_QB, 3 * A_HPG), lambda b, g, i: (b, g, i, 0)),
                  tab_spec, tab_spec, tab_spec,
                  pl.BlockSpec((NSA_SLOTS, NSA_SLOTS), lambda b, g, i: (0, 0)),
                  pl.BlockSpec((T // NSA_KT, NSA_SLOTS, NSA_KT), lambda b, g, i: (0, 0, 0))],
        out_specs=pl.BlockSpec((1, NSA_QB, A_HPG * A_HD), lambda b, g, i: (b, i, g)),
        out_shape=jax.ShapeDtypeStruct((B, T, A_HEADS * A_HD), f32),
        compiler_params=pltpu.CompilerParams(
            dimension_semantics=("parallel", "parallel", "arbitrary"),
            vmem_limit_bytes=VMEM_LIMIT_BYTES),
        name="nsa_attn_prompt",
    )(q, rows, rows, win, win, kc_p, vc_p, gate_g, cos_t, sna_t, snb_t, ovl, expand)


def nsa_project(x, pos, w_in):
    B, T, _ = x.shape
    qd, kvd = A_HEADS * A_HD, 6 * A_KV * A_HD
    q, kv, gate = jnp.split(mm(x, w_in), [qd, qd + kvd], axis=-1)
    q = q.reshape(B, T, A_HEADS, A_HD)
    kv = kv.reshape(B, T, 6, A_KV, A_HD)
    rows = jnp.stack([kv[:, :, 0], kv[:, :, 1], partial_rope(kv[:, :, 2], pos), kv[:, :, 3]], axis=2)
    win = jnp.stack([partial_rope(kv[:, :, 4], pos), kv[:, :, 5]], axis=2)
    gate = jax.nn.sigmoid(gate.reshape(B, T, A_HEADS, 3).astype(jnp.float32))
    return q, partial_rope(q, pos), rows, win, gate


def nsa_compress(kv_cmp, pe, w1, w2):
    B, T = kv_cmp.shape[:2]
    nc = (T - CMP_LEN) // CMP_STRIDE + 1
    idx = (jnp.arange(nc) * CMP_STRIDE)[:, None] + jnp.arange(CMP_LEN)[None, :]
    blk = kv_cmp[:, idx] + jnp.transpose(pe, (1, 0, 2))[:, :, None, :]
    flat = jnp.transpose(blk, (0, 1, 3, 4, 2, 5)).reshape(B, nc, 2, A_KV, CMP_LEN * A_HD)
    hid = jax.nn.gelu(jnp.einsum('bnckf,cfm->bnckm', flat, w1))
    out = jnp.einsum('bnckm,cmd->bnckd', hid, w2)
    return out[:, :, 0], out[:, :, 1]


def nsa_attend(q, q_rot, q_pos, kc, vc, ks, vs, kw, vw, kw_pos, gate):
    B, Qn = q.shape[:2]
    f32 = jnp.float32
    scale = A_HD ** -0.5
    qc = q.reshape(B, Qn, A_KV, A_HPG, A_HD).astype(f32) * scale
    qs = q_rot.reshape(B, Qn, A_KV, A_HPG, A_HD).astype(f32) * scale
    t = q_pos
    NC = kc.shape[1]
    c_start = jnp.arange(NC) * CMP_STRIDE
    c_mask = (c_start + CMP_LEN - 1)[None, :] <= t[:, None]
    p_c = masked_softmax(jnp.einsum('bqghd,bngd->bghqn', qc, kc), c_mask)
    o_c = jnp.einsum('bghqn,bngd->bqghd', p_c, vc)
    TK = ks.shape[1]
    NS = -(-TK // SEL_BLK)

    def blocks(a):
        a = jnp.pad(a, ((0, 0), (0, NS * SEL_BLK - TK), (0, 0), (0, 0)))
        return jnp.transpose(a.reshape(B, NS, SEL_BLK, A_KV, A_HD), (0, 3, 1, 2, 4))

    ks_b, vs_b = blocks(ks), blocks(vs)
    s_start = jnp.arange(NS) * SEL_BLK
    overlap = ((c_start[:, None] < s_start[None, :] + SEL_BLK) &
               (c_start[:, None] + CMP_LEN > s_start[None, :])).astype(f32)
    imp = jnp.einsum('bghqn,ns->bgqs', p_c, overlap)
    cur = (t // SEL_BLK)[:, None]
    j = jnp.arange(NS)[None, :]
    valid = s_start[None, :] <= t[:, None]
    forced = (j == 0) | (j == cur) | (j == cur - 1)
    score = jnp.where(valid, jnp.where(forced, FORCE_SCORE, imp), NEG_INF)
    n_top = min(SEL_TOP, NS)
    _, idx = lax.top_k(score, n_top)
    sel_ok = jnp.take_along_axis(jnp.broadcast_to(valid, score.shape), idx, axis=-1)
    bi = jnp.arange(B)[:, None, None, None]
    gi = jnp.arange(A_KV)[None, :, None, None]
    k_sel = ks_b[bi, gi, idx]
    v_sel = vs_b[bi, gi, idx].reshape(B, A_KV, Qn, n_top * SEL_BLK, A_HD)
    key_pos = idx[..., None] * SEL_BLK + jnp.arange(SEL_BLK)
    s_mask = (key_pos <= t[None, None, :, None, None]) & sel_ok[..., None]
    s_s = jnp.einsum('bqghd,bgqnkd->bghqnk', qs, k_sel).reshape(B, A_KV, A_HPG, Qn, n_top * SEL_BLK)
    p_s = masked_softmax(s_s, s_mask.reshape(B, A_KV, 1, Qn, n_top * SEL_BLK))
    o_s = jnp.einsum('bghqm,bgqmd->bqghd', p_s, v_sel)
    w_mask = ((kw_pos[None, :] <= t[:, None]) & (kw_pos[None, :] > t[:, None] - WINDOW) & (kw_pos[None, :] >= 0))
    p_w = masked_softmax(jnp.einsum('bqghd,bkgd->bghqk', qs, kw), w_mask)
    o_w = jnp.einsum('bghqk,bkgd->bqghd', p_w, vw)
    g = gate.reshape(B, Qn, A_KV, A_HPG, 3)
    o = g[..., 0:1] * o_c + g[..., 1:2] * o_s + g[..., 2:3] * o_w
    return o.reshape(B, Qn, A_HEADS * A_HD)


def nsa_prompt(x, w_in, pe, w1, w2, w_out):
    B, T, _ = x.shape
    pos = jnp.arange(T, dtype=jnp.int32)
    qd, kvd = A_HEADS * A_HD, 6 * A_KV * A_HD
    q, kv, gate_logits = jnp.split(mm(x, w_in), [qd, qd + kvd], axis=-1)
    kv = kv.reshape(B, T, 6, A_KV, A_HD)
    rows = jnp.stack([kv[:, :, 0], kv[:, :, 1], partial_rope(kv[:, :, 2], pos), kv[:, :, 3]], axis=2)
    win = jnp.stack([partial_rope(kv[:, :, 4], pos), kv[:, :, 5]], axis=2)
    kc, vc = nsa_compress(rows[:, :, 0:2], pe, w1, w2)
    o = nsa_attend_prompt(q, rows.reshape(B, T, 4 * A_KV * A_HD), win.reshape(B, T, 2 * A_KV * A_HD), kc, vc,
                          gate_logits)
    return mm(o, w_out), rows, win[:, T - min(WINDOW, T):]


def nsa_sample(x, cache_kv, page_table, win_buf, w_in, pe, w1, w2, w_out):
    B, S, _ = x.shape
    past = page_table.shape[1] * PAGE_SIZE
    pos = past + jnp.arange(S, dtype=jnp.int32)
    q, q_rot, rows, win, gate = nsa_project(x, pos, w_in)
    past_rows = cache_kv[page_table].reshape(B, past, 4, A_KV, A_HD)
    all_rows = jnp.concatenate([past_rows.astype(rows.dtype), rows], axis=1)
    kc, vc = nsa_compress(all_rows[:, :, 0:2], pe, w1, w2)
    wb_len = win_buf.shape[1]
    kw_all = jnp.concatenate([win_buf.astype(win.dtype), win], axis=1)
    kw_pos = past - wb_len + jnp.arange(wb_len + S, dtype=jnp.int32)
    o = nsa_attend(q, q_rot, pos, kc, vc, all_rows[:, :, 2], all_rows[:, :, 3],
                   kw_all[:, :, 0], kw_all[:, :, 1], kw_pos, gate)
    return mm(o, w_out), rows, kw_all[:, S:]


def kernel(x_prompt, x_sample, state_mlstm_C, state_mlstm_n, state_mlstm_m, state_rwkv_S, state_rwkv_shift,
           state_conv, cache_nsa_kv, state_nsa_win, page_table, norm_g, mlp_up, mlp_down,
           m_w_in, m_b_gate, m_norm, m_w_out, r_mix, r_vec, r_w_rkv, r_w1, r_w2, r_a1, r_a2, r_g1, r_g2, r_wo,
           c_w_in, c_conv, c_w_out, a_w_in, a_cmp_pe, a_cmp_w1, a_cmp_w2, a_w_out):
    xp, xs = x_prompt, x_sample
    Bp = xp.shape[0]
    dt = xp.dtype
    depth = norm_g.shape[0]
    for i in range(depth):
        kind = i % N_MIXERS
        hp, hs = rms_norm(xp, norm_g[i, 0]), rms_norm(xs, norm_g[i, 0])
        if kind == 0:
            yp, (mC_p, mn_p, mm_p) = mlstm_mixer(hp, jnp.zeros((Bp, M_HEADS, M_DQK, M_DV), dt),
                                                 jnp.zeros((Bp, M_HEADS, M_DQK), dt), jnp.zeros((Bp, M_HEADS), dt),
                                                 m_w_in, m_b_gate, m_norm, m_w_out)
            ys, (mC_s, mn_s, mm_s) = mlstm_mixer(hs, state_mlstm_C, state_mlstm_n, state_mlstm_m,
                                                 m_w_in, m_b_gate, m_norm, m_w_out)
        elif kind == 1:
            yp, (rS_p, rx_p) = rwkv7_mixer(hp, jnp.zeros((Bp, R_HEADS, R_HEAD, R_HEAD), dt), jnp.zeros((Bp, D_MODEL), dt),
                                           r_mix, r_vec, r_w_rkv, r_w1, r_w2, r_a1, r_a2, r_g1, r_g2, r_wo, True)
            ys, (rS_s, rx_s) = rwkv7_mixer(hs, state_rwkv_S, state_rwkv_shift,
                                           r_mix, r_vec, r_w_rkv, r_w1, r_w2, r_a1, r_a2, r_g1, r_g2, r_wo, False)
        elif kind == 2:
            yp, cv_p = shortconv_mixer(hp, jnp.zeros((Bp, CONV_W - 1, D_MODEL), dt), c_w_in, c_conv, c_w_out)
            ys, cv_s = shortconv_mixer(hs, state_conv, c_w_in, c_conv, c_w_out)
        else:
            yp, kv_p, win_p = nsa_prompt(hp, a_w_in, a_cmp_pe, a_cmp_w1, a_cmp_w2, a_w_out)
            ys, kv_s, win_s = nsa_sample(hs, cache_nsa_kv, page_table, state_nsa_win,
                                         a_w_in, a_cmp_pe, a_cmp_w1, a_cmp_w2, a_w_out)
        xp = xp + rms_norm(yp, norm_g[i, 1])
        xs = xs + rms_norm(ys, norm_g[i, 1])
        xp = xp + rms_norm(sq_relu_mlp(rms_norm(xp, norm_g[i, 2]), mlp_up[i], mlp_down[i]), norm_g[i, 3])
        xs = xs + rms_norm(sq_relu_mlp(rms_norm(xs, norm_g[i, 2]), mlp_up[i], mlp_down[i]), norm_g[i, 3])
    return (xp, xs, mC_p, mC_s, mn_p, mn_s, mm_p, mm_s, rS_p, rS_s, rx_p, rx_s, cv_p, cv_s, kv_p, kv_s, win_p, win_s)
```

```python
import functools
import math

import jax
import jax.numpy as jnp
from jax import lax
from jax.experimental import pallas as pl
from jax.experimental.pallas import tpu as pltpu

D_MODEL = 4096
N_MIXERS = 4
NORM_EPS = 1e-6
NEG_INF = -1e30
TINY = 1e-30

M_HEADS = 8
M_DV = D_MODEL // M_HEADS
M_DQK = M_DV // 2
M_CHUNK = 64
M_GATE_CAP = 15.0

R_HEAD = 64
R_HEADS = D_MODEL // R_HEAD
R_GN_EPS = 64e-5

CONV_W = 3

A_HEADS = 32
A_HD = D_MODEL // A_HEADS
A_KV = 4
A_HPG = A_HEADS // A_KV
ROPE_DIM = A_HD // 4
ROPE_THETA = 500000.0
CMP_LEN = 32
CMP_STRIDE = 16
SEL_BLK = 64
SEL_TOP = 16
WINDOW = 512
NSA_QBLK = 64
FORCE_SCORE = 1e9
PAGE_SIZE = 128

VMEM_LIMIT_BYTES = 52 * 1024 * 1024


def _epilogue(acc, epilogue):
    if epilogue == 'relu2':
        r = jnp.maximum(acc, 0.0)
        return r * r
    return acc


def _mm_fullk_kernel(x_ref, w_ref, o_ref, *, epilogue):
    acc = jnp.dot(x_ref[...], w_ref[...].astype(jnp.bfloat16), preferred_element_type=jnp.float32)
    o_ref[...] = _epilogue(acc, epilogue).astype(o_ref.dtype)


def _mm_splitk_kernel(x_ref, w_ref, o_ref):
    @pl.when(pl.program_id(2) == 0)
    def _():
        o_ref[...] = jnp.zeros_like(o_ref)

    o_ref[...] += jnp.dot(x_ref[...], w_ref[...].astype(jnp.bfloat16), preferred_element_type=jnp.float32)


def _pick(n, pref):
    for t in pref:
        if n % t == 0:
            return t
    return n


MM_FULLK_MAX = 4096


def mm(x, w, epilogue=None, out_dtype=jnp.float32):
    lead = x.shape[:-1]
    K = x.shape[-1]
    N = w.shape[1]
    x2 = x.reshape(-1, K).astype(jnp.bfloat16)
    M = x2.shape[0]
    tm = _pick(M, (2048, 1024, 512, 256, 128, 64, 32, 16))
    if K <= MM_FULLK_MAX:
        tn = 512 if N > 512 else N
        out = pl.pallas_call(
            functools.partial(_mm_fullk_kernel, epilogue=epilogue),
            grid=(M // tm, pl.cdiv(N, tn)),
            in_specs=[pl.BlockSpec((tm, K), lambda i, j: (i, 0), pipeline_mode=pl.Buffered(1)),
                      pl.BlockSpec((K, tn), lambda i, j: (0, j))],
            out_specs=pl.BlockSpec((tm, tn), lambda i, j: (i, j)),
            out_shape=jax.ShapeDtypeStruct((M, N), out_dtype),
            compiler_params=pltpu.CompilerParams(
                dimension_semantics=("parallel", "parallel"),
                vmem_limit_bytes=VMEM_LIMIT_BYTES),
            name="mm_fullk",
        )(x2, w)
    else:
        assert epilogue is None and out_dtype == jnp.float32
        tn = _pick(N, (1024, 512, 256, 128))
        tk = _pick(K, (1024, 512, 256, 128))
        out = pl.pallas_call(
            _mm_splitk_kernel,
            grid=(M // tm, N // tn, K // tk),
            in_specs=[pl.BlockSpec((tm, tk), lambda i, j, k: (i, k)),
                      pl.BlockSpec((tk, tn), lambda i, j, k: (k, j))],
            out_specs=pl.BlockSpec((tm, tn), lambda i, j, k: (i, j)),
            out_shape=jax.ShapeDtypeStruct((M, N), jnp.float32),
            compiler_params=pltpu.CompilerParams(
                dimension_semantics=("parallel", "parallel", "arbitrary"),
                vmem_limit_bytes=VMEM_LIMIT_BYTES),
            name="mm_splitk",
        )(x2, w)
    return out.reshape(*lead, N)


NORM_ROWS = 256


def _rms(x, g):
    return x * lax.rsqrt(jnp.mean(x * x, axis=-1, keepdims=True) + NORM_EPS) * g


def _prenorm_kernel(x_ref, g_ref, h_ref):
    h_ref[...] = _rms(x_ref[...], g_ref[...]).astype(h_ref.dtype)


def _resnorm_kernel(x_ref, y_ref, gp_ref, gn_ref, xo_ref, h_ref):
    x = x_ref[...] + _rms(y_ref[...], gp_ref[...])
    xo_ref[...] = x
    h_ref[...] = _rms(x, gn_ref[...]).astype(h_ref.dtype)


def _res_kernel(x_ref, y_ref, gp_ref, xo_ref):
    xo_ref[...] = x_ref[...] + _rms(y_ref[...], gp_ref[...])


def _rowwise_call(body, arrays, gains, out_dtypes, name):
    lead = arrays[0].shape[:-1]
    D = arrays[0].shape[-1]
    arrays = [a.reshape(-1, D) for a in arrays]
    M = arrays[0].shape[0]
    tr = _pick(M, (NORM_ROWS,))
    row_spec = pl.BlockSpec((tr, D), lambda i: (i, 0))
    g_spec = pl.BlockSpec((1, D), lambda i: (0, 0))
    outs = pl.pallas_call(
        body,
        grid=(M // tr,),
        in_specs=[row_spec] * len(arrays) + [g_spec] * len(gains),
        out_specs=[row_spec] * len(out_dtypes),
        out_shape=[jax.ShapeDtypeStruct((M, D), dt) for dt in out_dtypes],
        compiler_params=pltpu.CompilerParams(dimension_semantics=("parallel",), vmem_limit_bytes=VMEM_LIMIT_BYTES),
        name=name,
    )(*arrays, *[g.reshape(1, D).astype(jnp.float32) for g in gains])
    return [o.reshape(*lead, D) for o in outs]


def prenorm(x, g, h_dtype):
    return _rowwise_call(_prenorm_kernel, [x], [g], [h_dtype], "prenorm")[0]


def residual_norm(x, y, g_post, g_next, h_dtype):
    if g_next is None:
        return _rowwise_call(_res_kernel, [x, y], [g_post], [jnp.float32], "residual")[0], None
    return _rowwise_call(_resnorm_kernel, [x, y], [g_post, g_next], [jnp.float32, h_dtype], "residual_norm")


def sq_relu_mlp(x, w_up, w_down):
    h = mm(x, w_up, epilogue='relu2', out_dtype=jnp.bfloat16)
    return mm(h, w_down)


def masked_softmax(s, mask):
    s = jnp.where(mask, s.astype(jnp.float32), NEG_INF)
    m = jnp.max(s, axis=-1, keepdims=True)
    e = jnp.where(mask, jnp.exp(s - m), 0.0)
    return e / jnp.maximum(jnp.sum(e, axis=-1, keepdims=True), TINY)


def partial_rope(x, pos):
    half = ROPE_DIM // 2
    inv = ROPE_THETA ** (-jnp.arange(half, dtype=jnp.float32) / half)
    ang = pos.astype(jnp.float32)[:, None] * inv[None, :]
    cos, sin = jnp.cos(ang)[:, None, :], jnp.sin(ang)[:, None, :]
    xr = x[..., :ROPE_DIM].astype(jnp.float32)
    x1, x2 = xr[..., :half], xr[..., half:]
    rot = jnp.concatenate([x1 * cos - x2 * sin, x2 * cos + x1 * sin], axis=-1).astype(x.dtype)
    return jnp.concatenate([rot, x[..., ROPE_DIM:]], axis=-1)


def _softcap(z):
    return M_GATE_CAP * jnp.tanh(z / M_GATE_CAP)


def mlstm_mixer(x, C0, n0, m0, w_in, b_gate, g_norm, w_out):
    B, T, _ = x.shape
    f32 = jnp.float32
    qd, vd = M_HEADS * M_DQK, M_HEADS * M_DV
    q, k, v, o, ig, fg = jnp.split(mm(x, w_in), [qd, 2 * qd, 2 * qd + vd, 2 * qd + 2 * vd, 2 * qd + 2 * vd + M_HEADS], axis=-1)
    q = q.reshape(B, T, M_HEADS, M_DQK).astype(f32)
    k = k.reshape(B, T, M_HEADS, M_DQK).astype(f32) * (M_DQK ** -0.5)
    v = v.reshape(B, T, M_HEADS, M_DV).astype(f32)
    log_i = _softcap((ig + b_gate[0]).astype(f32))
    log_f = jax.nn.log_sigmoid(_softcap((fg + b_gate[1]).astype(f32)))
    L = math.gcd(T, M_CHUNK)
    nc = T // L

    def chunks(a):
        a = a.reshape(B, nc, L, M_HEADS, *a.shape[3:])
        return jnp.moveaxis(a, (1, 3), (0, 2))

    causal = jnp.tril(jnp.ones((L, L), dtype=bool))

    def step(carry, inp):
        C, n, m = carry
        qc, kc, vc, li, lf = inp
        b = jnp.cumsum(lf, axis=-1)
        dlog = jnp.where(causal, b[..., :, None] - b[..., None, :] + li[..., None, :], -jnp.inf)
        m_inter = b + m[..., None]
        m_s = jnp.maximum(m_inter, jnp.max(dlog, axis=-1))
        a = jnp.exp(dlog - m_s[..., None]) * jnp.einsum('bhsd,bhrd->bhsr', qc, kc)
        w_inter = jnp.exp(m_inter - m_s)
        num = w_inter[..., None] * jnp.einsum('bhsd,bhde->bhse', qc, C) + jnp.einsum('bhsr,bhre->bhse', a, vc)
        den = w_inter * jnp.einsum('bhsd,bhd->bhs', qc, n) + jnp.sum(a, axis=-1)
        h = num / jnp.maximum(jnp.abs(den), jnp.exp(-m_s))[..., None]
        b_last = b[..., -1]
        g_r = b_last[..., None] - b + li
        m_new = jnp.maximum(b_last + m, jnp.max(g_r, axis=-1))
        w_r = jnp.exp(g_r - m_new[..., None])
        w_c = jnp.exp(b_last + m - m_new)
        C_new = w_c[..., None, None] * C + jnp.einsum('bhr,bhrd,bhre->bhde', w_r, kc, vc)
        n_new = w_c[..., None] * n + jnp.einsum('bhr,bhrd->bhd', w_r, kc)
        return (C_new, n_new, m_new), h

    (C, n, m), h = lax.scan(step, (C0.astype(f32), n0.astype(f32), m0.astype(f32)),
                            (chunks(q), chunks(k), chunks(v), chunks(log_i), chunks(log_f)))
    h = jnp.moveaxis(h, (0, 2), (1, 3)).reshape(B, T, M_HEADS, M_DV)
    h = h * lax.rsqrt(jnp.mean(h * h, axis=-1, keepdims=True) + NORM_EPS) * g_norm
    h = h * jax.nn.sigmoid(o.reshape(B, T, M_HEADS, M_DV).astype(f32))
    y = mm(h.reshape(B, T, M_HEADS * M_DV), w_out)
    return y, (C.astype(C0.dtype), n.astype(n0.dtype), m.astype(m0.dtype))


def _mlstm_chunk_kernel(q_ref, k_ref, v_ref, o_ref, igc_ref, fgc_ref, igr_ref, fgr_ref, gn_ref, c0_ref, n0_ref, m0_ref,
                        y_ref, cout_ref, nout_ref, mout_ref, c_scr, n_scr, m_scr, *, nchunks):
    f32, bf16 = jnp.float32, jnp.bfloat16
    c = pl.program_id(2)

    @pl.when(c == 0)
    def _():
        c_scr[...] = c0_ref[0, 0]
        n_scr[...] = n0_ref[0, 0]
        m_scr[...] = m0_ref[0, 0]

    def softcap(z):
        return M_GATE_CAP * jnp.tanh(z / M_GATE_CAP)

    def log_sigmoid(z):
        return jnp.minimum(z, 0.0) - jnp.log(1.0 + jnp.exp(-jnp.abs(z)))

    L = q_ref.shape[0]
    q = q_ref[...]
    k = k_ref[...] * (M_DQK ** -0.5)
    v = v_ref[...].astype(bf16)
    li_c, li_r = softcap(igc_ref[0, 0, 0]), softcap(igr_ref[0, 0, 0])
    lf_c, lf_r = log_sigmoid(softcap(fgc_ref[0, 0, 0])), log_sigmoid(softcap(fgr_ref[0, 0, 0]))
    s_idx = lax.broadcasted_iota(jnp.int32, (L, L), 0)
    r_idx = lax.broadcasted_iota(jnp.int32, (L, L), 1)
    causal = r_idx <= s_idx
    b_c = jnp.sum(jnp.where(causal, lf_r, 0.0), axis=1, keepdims=True)
    b_r = jnp.sum(jnp.where(s_idx <= r_idx, lf_c, 0.0), axis=0, keepdims=True)
    m = m_scr[...]
    dlog = jnp.where(causal, b_c - b_r + li_r, -jnp.inf)
    m_inter = b_c + m
    m_s = jnp.maximum(m_inter, jnp.max(dlog, axis=1, keepdims=True))
    q16 = q.astype(bf16)
    a = jnp.exp(dlog - m_s) * _dot_nt(q16, k.astype(bf16))
    w_inter = jnp.exp(m_inter - m_s)
    num = (w_inter * jnp.dot(q16, c_scr[...].astype(bf16), preferred_element_type=f32)
           + jnp.dot(a.astype(bf16), v, preferred_element_type=f32))
    den = w_inter * jnp.sum(q * n_scr[...], axis=1, keepdims=True) + jnp.sum(a, axis=1, keepdims=True)
    h = num / jnp.maximum(jnp.abs(den), jnp.exp(-m_s))
    b_last = b_r[:, L - 1:L]
    m_new = jnp.maximum(b_last + m, jnp.max(b_last - b_r + li_r, axis=1, keepdims=True))
    w_c = jnp.exp(b_last + m - m_new)
    kw = k * jnp.exp(b_last - b_c + li_c - m_new)
    c_scr[...] = w_c * c_scr[...] + jnp.dot(kw.T.astype(bf16), v, preferred_element_type=f32)
    n_scr[...] = w_c * n_scr[...] + jnp.sum(kw, axis=0, keepdims=True)
    m_scr[...] = m_new

    hn = h * lax.rsqrt(jnp.mean(h * h, axis=1, keepdims=True) + NORM_EPS) * gn_ref[0]
    y_ref[...] = (hn * (1.0 / (1.0 + jnp.exp(-o_ref[...])))).astype(y_ref.dtype)

    @pl.when(c == nchunks - 1)
    def _():
        cout_ref[0, 0] = c_scr[...]
        nout_ref[0, 0] = n_scr[...]
        mout_ref[0, 0] = m_scr[...]


def mlstm_chunked(proj, b_gate, g_norm, C0, n0, m0, B, T):
    f32 = jnp.float32
    L = math.gcd(T, M_CHUNK)
    nc = T // L
    assert L % 16 == 0
    gates = proj[:, 2 * M_HEADS * M_DQK + 2 * M_HEADS * M_DV:].reshape(B, nc, L, 2, M_HEADS) + b_gate
    g_col = jnp.transpose(gates, (3, 0, 4, 1, 2))[..., None]
    g_row = jnp.transpose(gates, (3, 0, 4, 1, 2))[..., None, :]
    qk_blk = M_HEADS * M_DQK // M_DQK
    v_blk = 2 * M_HEADS * M_DQK // M_DV
    col_spec = pl.BlockSpec((1, 1, 1, L, 1), lambda b, h, c: (b, h, c, 0, 0))
    row_spec = pl.BlockSpec((1, 1, 1, 1, L), lambda b, h, c: (b, h, c, 0, 0))
    c_spec = pl.BlockSpec((1, 1, M_DQK, M_DV), lambda b, h, c: (b, h, 0, 0))
    n_spec = pl.BlockSpec((1, 1, 1, M_DQK), lambda b, h, c: (b, h, 0, 0))
    m_spec = pl.BlockSpec((1, 1, 1, 1), lambda b, h, c: (b, h, 0, 0))
    y, C, n, m = pl.pallas_call(
        functools.partial(_mlstm_chunk_kernel, nchunks=nc),
        grid=(B, M_HEADS, nc),
        in_specs=[pl.BlockSpec((L, M_DQK), lambda b, h, c: (b * nc + c, h)),
                  pl.BlockSpec((L, M_DQK), lambda b, h, c: (b * nc + c, qk_blk + h)),
                  pl.BlockSpec((L, M_DV), lambda b, h, c: (b * nc + c, v_blk + h)),
                  pl.BlockSpec((L, M_DV), lambda b, h, c: (b * nc + c, v_blk + M_HEADS + h)),
                  col_spec, col_spec, row_spec, row_spec,
                  pl.BlockSpec((1, 1, M_DV), lambda b, h, c: (h, 0, 0)),
                  c_spec, n_spec, m_spec],
        out_specs=[pl.BlockSpec((L, M_DV), lambda b, h, c: (b * nc + c, h)), c_spec, n_spec, m_spec],
        out_shape=[jax.ShapeDtypeStruct((B * T, M_HEADS * M_DV), jnp.bfloat16),
                   jax.ShapeDtypeStruct((B, M_HEADS, M_DQK, M_DV), f32),
                   jax.ShapeDtypeStruct((B, M_HEADS, 1, M_DQK), f32),
                   jax.ShapeDtypeStruct((B, M_HEADS, 1, 1), f32)],
        scratch_shapes=[pltpu.VMEM((M_DQK, M_DV), f32), pltpu.VMEM((1, M_DQK), f32), pltpu.VMEM((1, 1), f32)],
        compiler_params=pltpu.CompilerParams(
            dimension_semantics=("parallel", "parallel", "arbitrary"),
            vmem_limit_bytes=VMEM_LIMIT_BYTES),
        name="mlstm_chunk",
    )(proj, proj, proj, proj, g_col[0], g_col[1], g_row[0], g_row[1], g_norm.reshape(M_HEADS, 1, M_DV).astype(f32),
      C0.astype(f32), n0.astype(f32).reshape(B, M_HEADS, 1, M_DQK), m0.astype(f32).reshape(B, M_HEADS, 1, 1))
    return y, C, n.reshape(B, M_HEADS, M_DQK), m.reshape(B, M_HEADS)


def mlstm_prompt(x, C0, n0, m0, w_in, b_gate, g_norm, w_out):
    B, T, D = x.shape
    proj = mm(x.reshape(B * T, D), w_in)
    h, C, n, m = mlstm_chunked(proj, b_gate, g_norm, C0, n0, m0, B, T)
    return mm(h, w_out).reshape(B, T, D), (C.astype(C0.dtype), n.astype(n0.dtype), m.astype(m0.dtype))


RWKV_TC = 32
RWKV_ROWS = 8
RWKV_LANES = 128


def _rwkv_scan_kernel(r_ref, k_ref, v_ref, wp_ref, ap_ref, par_ref, s0_ref, y_ref, sout_ref,
                      s_scr, w_scr, b_scr, kx_scr, nkk_scr, y_scr, *, tc, nchunks, rows):
    c = pl.program_id(1)

    @pl.when(c == 0)
    def _():
        s_scr[...] = s0_ref[0]

    def prep(t, carry):
        kt = k_ref[0, t]
        z = -wp_ref[0, t]
        softplus = jnp.maximum(z, 0.0) + jnp.log(1.0 + jnp.exp(-jnp.abs(z)))
        w_scr[t] = jnp.exp(-jnp.exp(-softplus - 0.5))
        a = 1.0 / (1.0 + jnp.exp(-ap_ref[0, t]))
        kk = kt * par_ref[0]
        kk = kk / jnp.maximum(jnp.sqrt(jnp.sum(kk * kk, axis=0, keepdims=True)), 1e-12)
        b_scr[t] = kk * a
        nkk_scr[t] = -kk
        kx_scr[t] = kt * (1.0 + (a - 1.0) * par_ref[1])
        return carry

    lax.fori_loop(0, tc, prep, 0)

    def vblock(vb, carry):
        v0 = vb * rows

        def step(t, inner):
            for i in range(rows):
                S = s_scr[v0 + i]
                sa = jnp.sum(S * nkk_scr[t], axis=0, keepdims=True)
                vrow = v_ref[0, t, pl.ds(v0 + i, 1), :]
                S = S * w_scr[t] + sa * b_scr[t] + vrow * kx_scr[t]
                s_scr[v0 + i] = S
                y_scr[t, pl.ds(v0 + i, 1), :] = jnp.sum(S * r_ref[0, t], axis=0, keepdims=True)
            return inner

        lax.fori_loop(0, tc, step, 0)
        return carry

    lax.fori_loop(0, R_HEAD // rows, vblock, 0)

    def post(t, carry):
        y = y_scr[t]
        mu = jnp.mean(y, axis=0, keepdims=True)
        var = jnp.mean((y - mu) ** 2, axis=0, keepdims=True)
        yn = ((y - mu) * lax.rsqrt(var + R_GN_EPS)) * par_ref[3] + par_ref[4]
        bonus = jnp.sum(r_ref[0, t] * kx_scr[t] * par_ref[2], axis=0, keepdims=True) * v_ref[0, t]
        y_ref[0, t] = yn + bonus
        return carry

    lax.fori_loop(0, tc, post, 0)

    @pl.when(c == nchunks - 1)
    def _():
        sout_ref[0] = s_scr[...]


def _to_chain(z, B, T):
    z = z.reshape(B // 2, 2, T, R_HEADS, R_HEAD)
    return jnp.transpose(z, (0, 2, 4, 1, 3)).reshape(B // 2, T, R_HEAD, RWKV_LANES)


def _vec_to_chain(vec):
    return jnp.tile(vec.reshape(R_HEADS, R_HEAD).T, (1, 2))


def rwkv_scan(r, k, v, w_pre, a_pre, S0, k_k, k_a, r_k, gn_w, gn_b):
    B, T, D = r.shape
    assert B % 2 == 0 and 2 * R_HEADS == RWKV_LANES and T % RWKV_TC == 0
    G = B // 2
    nchunks = T // RWKV_TC
    f32 = jnp.float32
    ins = [_to_chain(z.astype(f32), B, T) for z in (r, k, v, w_pre, a_pre)]
    par = jnp.stack([_vec_to_chain(p.astype(f32)) for p in (k_k, k_a, r_k, gn_w, gn_b)])
    s0 = jnp.transpose(S0.astype(f32).reshape(G, 2, R_HEADS, R_HEAD, R_HEAD), (0, 3, 4, 1, 2))
    s0 = s0.reshape(G, R_HEAD, R_HEAD, RWKV_LANES)
    seq_spec = pl.BlockSpec((1, RWKV_TC, R_HEAD, RWKV_LANES), lambda g, c: (g, c, 0, 0))
    st_spec = pl.BlockSpec((1, R_HEAD, R_HEAD, RWKV_LANES), lambda g, c: (g, 0, 0, 0))
    seq_scr = pltpu.VMEM((RWKV_TC, R_HEAD, RWKV_LANES), f32)
    y, s_out = pl.pallas_call(
        functools.partial(_rwkv_scan_kernel, tc=RWKV_TC, nchunks=nchunks, rows=RWKV_ROWS),
        grid=(G, nchunks),
        in_specs=[seq_spec] * 5 + [pl.BlockSpec((5, R_HEAD, RWKV_LANES), lambda g, c: (0, 0, 0)), st_spec],
        out_specs=[seq_spec, st_spec],
        out_shape=[jax.ShapeDtypeStruct((G, T, R_HEAD, RWKV_LANES), f32),
                   jax.ShapeDtypeStruct((G, R_HEAD, R_HEAD, RWKV_LANES), f32)],
        scratch_shapes=[pltpu.VMEM((R_HEAD, R_HEAD, RWKV_LANES), f32)] + [seq_scr] * 5,
        compiler_params=pltpu.CompilerParams(
            dimension_semantics=("parallel", "arbitrary"),
            vmem_limit_bytes=VMEM_LIMIT_BYTES),
        name="rwkv_scan",
    )(*ins, par, s0)
    y = jnp.transpose(y.reshape(G, T, R_HEAD, 2, R_HEADS), (0, 3, 1, 4, 2)).reshape(B, T, D)
    s_out = jnp.transpose(s_out.reshape(G, R_HEAD, R_HEAD, 2, R_HEADS), (0, 3, 4, 1, 2))
    return y, s_out.reshape(B, R_HEADS, R_HEAD, R_HEAD)


def rwkv7_mixer(x, S0, shift0, mix, vec, w_rkv, w1, w2, a1, a2, g1, g2, wo, use_scan_kernel):
    B, T, D = x.shape
    f32 = jnp.float32
    bf16 = jnp.bfloat16
    xx = jnp.concatenate([shift0[:, None, :].astype(x.dtype), x[:, :-1]], axis=1) - x

    def lerp(j):
        return (x + xx * mix[j]).astype(bf16)

    w0, a0, k_k, k_a, r_k, gn_w, gn_b = vec[0], vec[1], vec[2], vec[3], vec[4], vec[5], vec[6]
    r = mm(lerp(0), w_rkv[0])
    k = mm(lerp(2), w_rkv[1])
    v = mm(lerp(3), w_rkv[2])
    w_pre = (w0 + mm(jnp.tanh(mm(lerp(1), w1)).astype(bf16), w2)).astype(f32)
    a_pre = (a0 + mm(mm(lerp(4), a1).astype(bf16), a2)).astype(f32)
    g = mm(jax.nn.sigmoid(mm(lerp(5), g1)).astype(bf16), g2)
    if use_scan_kernel:
        y, S = rwkv_scan(r, k, v, w_pre, a_pre, S0, k_k, k_a, r_k, gn_w, gn_b)
        return mm((y * g).astype(bf16), wo), (S.astype(S0.dtype), x[:, -1])

    w_log = -jax.nn.softplus(-w_pre) - 0.5
    decay = jnp.exp(-jnp.exp(w_log))
    a = jax.nn.sigmoid(a_pre)

    def heads(z):
        return z.reshape(B, T, R_HEADS, R_HEAD).astype(f32)

    kk = heads(k * k_k)
    kk = kk / jnp.maximum(jnp.sqrt(jnp.sum(kk * kk, axis=-1, keepdims=True)), 1e-12)
    k = k.astype(f32) * (1.0 + (a - 1.0) * k_a)
    r_h, k_h, v_h, a_h, w_h = heads(r), heads(k), heads(v), heads(a), heads(decay)

    def step(S, inp):
        r_t, w_t, k_t, v_t, kk_t, a_t = inp
        sa = jnp.sum(S * (-kk_t)[:, :, None, :], axis=-1)
        S = S * w_t[:, :, None, :] + sa[..., None] * (kk_t * a_t)[:, :, None, :] + v_t[..., None] * k_t[:, :, None, :]
        return S, jnp.sum(S * r_t[:, :, None, :], axis=-1)

    S, y = lax.scan(step, S0.astype(f32),
                    (jnp.moveaxis(r_h, 1, 0), jnp.moveaxis(w_h, 1, 0), jnp.moveaxis(k_h, 1, 0),
                     jnp.moveaxis(v_h, 1, 0), jnp.moveaxis(kk, 1, 0), jnp.moveaxis(a_h, 1, 0)))
    y = jnp.moveaxis(y, 0, 1)
    mu = jnp.mean(y, axis=-1, keepdims=True)
    var = jnp.mean((y - mu) ** 2, axis=-1, keepdims=True)
    y = ((y - mu) * lax.rsqrt(var + R_GN_EPS)).reshape(B, T, D) * gn_w + gn_b
    bonus = jnp.sum(r_h * k_h * r_k.reshape(R_HEADS, R_HEAD), axis=-1, keepdims=True) * v_h
    y = (y + bonus.reshape(B, T, D)) * g
    return mm(y.astype(bf16), wo), (S.astype(S0.dtype), x[:, -1])


def shortconv_mixer(x, buf, w_in, conv_w, w_out):
    T = x.shape[1]
    b_gate, c_gate, u = jnp.split(mm(x, w_in), 3, axis=-1)
    z = jnp.concatenate([buf.astype(u.dtype), c_gate * u], axis=1)
    conv = sum(z[:, j:j + T] * conv_w[j] for j in range(CONV_W))
    return mm(b_gate * conv, w_out), z[:, T:]


NSA_QB = 128
NSA_KT = 512
NSA_WT = WINDOW + NSA_QB
NSA_SLOTS = 128


def _dot_nt(a, b):
    return lax.dot_general(a, b, (((1,), (1,)), ((), ())), preferred_element_type=jnp.float32)


def _softmax_rows(s, mask):
    s = jnp.where(mask, s, NEG_INF)
    m = jnp.max(s, axis=-1, keepdims=True)
    e = jnp.where(mask, jnp.exp(s - m), 0.0)
    return e / jnp.maximum(jnp.sum(e, axis=-1, keepdims=True), TINY)


def _nsa_attn_kernel(q_ref, ks_ref, vs_ref, kw_ref, vw_ref, kc_ref, vc_ref, gate_ref, cos_ref, sna_ref, snb_ref,
                     ovl_ref, exp_ref, o_ref, *, qb, seq, n_cmp, n_sel):
    f32, bf16 = jnp.float32, jnp.bfloat16
    t0 = pl.program_id(2) * qb
    scale = A_HD ** -0.5
    cos, sna, snb = cos_ref[...], sna_ref[...], snb_ref[...]

    def rope(xh):
        return xh * cos + pltpu.roll(xh, A_HD - ROPE_DIM // 2, 1) * sna + pltpu.roll(xh, ROPE_DIM // 2, 1) * snb

    heads = [q_ref[0, :, h * A_HD:(h + 1) * A_HD] for h in range(A_HPG)]
    qc = jnp.concatenate([xh * scale for xh in heads], axis=0)
    qs = jnp.concatenate([(rope(xh) * scale).astype(bf16) for xh in heads], axis=0)
    rows = A_HPG * qb
    tq = t0 + lax.broadcasted_iota(jnp.int32, (qb, 1), 0)
    tq3 = tq[None]

    lane = lax.broadcasted_iota(jnp.int32, (1, 1, NSA_SLOTS), 2)
    kc = kc_ref[0]
    qc_hi, kc_hi = qc.astype(bf16), kc.astype(bf16)
    qc_lo, kc_lo = (qc - qc_hi.astype(f32)).astype(bf16), (kc - kc_hi.astype(f32)).astype(bf16)
    s_c = (_dot_nt(qc_hi, kc_hi) + _dot_nt(qc_hi, kc_lo) + _dot_nt(qc_lo, kc_hi)).reshape(A_HPG, qb, NSA_SLOTS)
    c_mask = (lane * CMP_STRIDE + (CMP_LEN - 1) <= tq3) & (lane < n_cmp)
    p_c = _softmax_rows(s_c, c_mask)
    o_c = jnp.dot(p_c.reshape(rows, NSA_SLOTS).astype(bf16), vc_ref[0].astype(bf16), preferred_element_type=f32)
    psum = jnp.sum(p_c, axis=0)
    ovl = ovl_ref[...]
    hi = psum.astype(bf16)
    r1 = psum - hi.astype(f32)
    mid = r1.astype(bf16)
    lo = (r1 - mid.astype(f32)).astype(bf16)
    imp = (jnp.dot(hi, ovl, preferred_element_type=f32) + jnp.dot(mid, ovl, preferred_element_type=f32)
           + jnp.dot(lo, ovl, preferred_element_type=f32))

    sidx = lax.broadcasted_iota(jnp.int32, (qb, NSA_SLOTS), 1)
    cur = lax.shift_right_logical(tq, SEL_BLK.bit_length() - 1)
    valid = (sidx * SEL_BLK <= tq) & (sidx < n_sel)
    forced = (sidx == 0) | (sidx == cur) | (sidx == cur - 1)
    score = jnp.where(valid, jnp.where(forced, FORCE_SCORE, imp), NEG_INF)
    rank = jnp.zeros((qb, NSA_SLOTS), jnp.int32)
    for j in range(n_sel):
        cj = score[:, j:j + 1]
        beats = (cj > score) | ((cj == score) & (sidx > j))
        rank = rank + beats.astype(jnp.int32)
    sel = ((rank < SEL_TOP) & valid).astype(bf16)

    kpos0 = lax.broadcasted_iota(jnp.int32, (qb, NSA_KT), 1)

    def sel_tile(kt, carry):
        m, l, acc = carry
        k0 = pl.multiple_of(kt * NSA_KT, NSA_KT)
        s = _dot_nt(qs, ks_ref[0, pl.ds(k0, NSA_KT), :].astype(bf16)).reshape(A_HPG, qb, NSA_KT)
        picked = jnp.dot(sel, exp_ref[kt], preferred_element_type=f32) > 0.5
        mask = (picked & (kpos0 + k0 <= tq))[None]
        s = jnp.where(mask, s, NEG_INF)
        m_new = jnp.maximum(m, jnp.max(s, axis=-1, keepdims=True))
        alpha = jnp.exp(m - m_new)
        e = jnp.where(mask, jnp.exp(s - m_new), 0.0)
        l = alpha * l + jnp.sum(e, axis=-1, keepdims=True)
        pv = jnp.dot(e.reshape(rows, NSA_KT).astype(bf16), vs_ref[0, pl.ds(k0, NSA_KT), :].astype(bf16),
                     preferred_element_type=f32)
        acc = alpha * acc + pv.reshape(A_HPG, qb, A_HD)
        return m_new, l, acc

    n_tiles = (t0 + qb + NSA_KT - 1) // NSA_KT
    m0 = jnp.full((A_HPG, qb, 1), NEG_INF, f32)
    _, l_s, acc_s = lax.fori_loop(0, n_tiles, sel_tile,
                                  (m0, jnp.zeros((A_HPG, qb, 1), f32), jnp.zeros((A_HPG, qb, A_HD), f32)))
    o_s = acc_s / jnp.maximum(l_s, TINY)

    w0 = pl.multiple_of(jnp.clip(t0 + qb - NSA_WT, 0, seq - NSA_WT), qb)
    s_w = _dot_nt(qs, kw_ref[0, pl.ds(w0, NSA_WT), :].astype(bf16)).reshape(A_HPG, qb, NSA_WT)
    wpos = w0 + lax.broadcasted_iota(jnp.int32, (1, 1, NSA_WT), 2)
    w_mask = (wpos <= tq3) & (wpos > tq3 - WINDOW)
    p_w = _softmax_rows(s_w, w_mask)
    o_w = jnp.dot(p_w.reshape(rows, NSA_WT).astype(bf16), vw_ref[0, pl.ds(w0, NSA_WT), :].astype(bf16),
                  preferred_element_type=f32).reshape(A_HPG, qb, A_HD)

    o_c = o_c.reshape(A_HPG, qb, A_HD)
    gate = 1.0 / (1.0 + jnp.exp(-gate_ref[0, 0]))
    for h in range(A_HPG):
        o_ref[0, :, h * A_HD:(h + 1) * A_HD] = (gate[:, 3 * h:3 * h + 1] * o_c[h]
                                                + gate[:, 3 * h + 1:3 * h + 2] * o_s[h]
                                                + gate[:, 3 * h + 2:3 * h + 3] * o_w[h])


def nsa_attend_prompt(q, rows, win, kc, vc, gate_logits):
    B, T, _ = q.shape
    f32, bf16 = jnp.float32, jnp.bfloat16
    assert T % NSA_QB == 0 and T % NSA_KT == 0 and T >= NSA_WT and T % SEL_BLK == 0
    n_cmp = kc.shape[1]
    n_sel = T // SEL_BLK
    assert n_cmp <= NSA_SLOTS and n_sel <= NSA_SLOTS
    pad = ((0, 0), (0, NSA_SLOTS - n_cmp), (0, 0), (0, 0))
    kc_p = jnp.pad(kc, pad).reshape(B, NSA_SLOTS, A_KV * A_HD)
    vc_p = jnp.pad(vc, pad).reshape(B, NSA_SLOTS, A_KV * A_HD)
    gate_g = jnp.transpose(gate_logits.reshape(B, T, A_KV, 3 * A_HPG), (0, 2, 1, 3)).astype(f32)
    half = ROPE_DIM // 2
    inv = ROPE_THETA ** (-jnp.arange(half, dtype=f32) / half)
    ang = jnp.arange(T, dtype=jnp.int32).astype(f32)[:, None] * inv[None, :]
    zeros = jnp.zeros((T, A_HD - ROPE_DIM), f32)
    cos_t = jnp.concatenate([jnp.cos(ang), jnp.cos(ang), 1.0 + zeros], axis=1)
    sna_t = jnp.concatenate([-jnp.sin(ang), jnp.zeros((T, half), f32), zeros], axis=1)
    snb_t = jnp.concatenate([jnp.zeros((T, half), f32), jnp.sin(ang), zeros], axis=1)
    c_start = jnp.arange(NSA_SLOTS) * CMP_STRIDE
    s_start = jnp.arange(NSA_SLOTS) * SEL_BLK
    ovl = ((c_start[:, None] < s_start[None, :] + SEL_BLK) & (c_start[:, None] + CMP_LEN > s_start[None, :])
           & (jnp.arange(NSA_SLOTS)[:, None] < n_cmp) & (jnp.arange(NSA_SLOTS)[None, :] < n_sel)).astype(bf16)
    expand = (jnp.arange(NSA_SLOTS)[:, None] == (jnp.arange(T)[None, :] // SEL_BLK)).astype(bf16)
    expand = jnp.transpose(expand.reshape(NSA_SLOTS, T // NSA_KT, NSA_KT), (1, 0, 2))

    def kv_spec(col0):
        return pl.BlockSpec((1, T, A_HD), lambda b, g, i: (b, 0, col0 + g))

    tab_spec = pl.BlockSpec((NSA_QB, A_HD), lambda b, g, i: (i, 0))
    return pl.pallas_call(
        functools.partial(_nsa_attn_kernel, qb=NSA_QB, seq=T, n_cmp=n_cmp, n_sel=n_sel),
        grid=(B, A_KV, T // NSA_QB),
        in_specs=[pl.BlockSpec((1, NSA_QB, A_HPG * A_HD), lambda b, g, i: (b, i, g)),
                  kv_spec(2 * A_KV), kv_spec(3 * A_KV), kv_spec(0), kv_spec(A_KV),
                  pl.BlockSpec((1, NSA_SLOTS, A_HD), lambda b, g, i: (b, 0, g)),
                  pl.BlockSpec((1, NSA_SLOTS, A_HD), lambda b, g, i: (b, 0, g)),
                  pl.BlockSpec((1, 1, NSA_QB, 3 * A_HPG), lambda b, g, i: (b, g, i, 0)),
                  tab_spec, tab_spec, tab_spec,
                  pl.BlockSpec((NSA_SLOTS, NSA_SLOTS), lambda b, g, i: (0, 0)),
                  pl.BlockSpec((T // NSA_KT, NSA_SLOTS, NSA_KT), lambda b, g, i: (0, 0, 0))],
        out_specs=pl.BlockSpec((1, NSA_QB, A_HPG * A_HD), lambda b, g, i: (b, i, g)),
        out_shape=jax.ShapeDtypeStruct((B, T, A_HEADS * A_HD), f32),
        compiler_params=pltpu.CompilerParams(
            dimension_semantics=("parallel", "parallel", "arbitrary"),
            vmem_limit_bytes=VMEM_LIMIT_BYTES),
        name="nsa_attn_prompt",
    )(q, rows, rows, win, win, kc_p, vc_p, gate_g, cos_t, sna_t, snb_t, ovl, expand)


def nsa_project(x, pos, w_in):
    B, T, _ = x.shape
    qd, kvd = A_HEADS * A_HD, 6 * A_KV * A_HD
    q, kv, gate = jnp.split(mm(x, w_in), [qd, qd + kvd], axis=-1)
    q = q.reshape(B, T, A_HEADS, A_HD)
    kv = kv.reshape(B, T, 6, A_KV, A_HD)
    rows = jnp.stack([kv[:, :, 0], kv[:, :, 1], partial_rope(kv[:, :, 2], pos), kv[:, :, 3]], axis=2)
    win = jnp.stack([partial_rope(kv[:, :, 4], pos), kv[:, :, 5]], axis=2)
    gate = jax.nn.sigmoid(gate.reshape(B, T, A_HEADS, 3).astype(jnp.float32))
    return q, partial_rope(q, pos), rows, win, gate


def nsa_compress(kv_cmp, pe, w1, w2):
    B, T = kv_cmp.shape[:2]
    nc = (T - CMP_LEN) // CMP_STRIDE + 1
    idx = (jnp.arange(nc) * CMP_STRIDE)[:, None] + jnp.arange(CMP_LEN)[None, :]
    blk = kv_cmp[:, idx] + jnp.transpose(pe, (1, 0, 2))[:, :, None, :]
    flat = jnp.transpose(blk, (0, 1, 3, 4, 2, 5)).reshape(B, nc, 2, A_KV, CMP_LEN * A_HD)
    hid = jax.nn.gelu(jnp.einsum('bnckf,cfm->bnckm', flat, w1))
    out = jnp.einsum('bnckm,cmd->bnckd', hid, w2)
    return out[:, :, 0], out[:, :, 1]


def nsa_attend(q, q_rot, q_pos, kc, vc, ks, vs, kw, vw, kw_pos, gate):
    B, Qn = q.shape[:2]
    f32 = jnp.float32
    scale = A_HD ** -0.5
    qc = q.reshape(B, Qn, A_KV, A_HPG, A_HD).astype(f32) * scale
    qs = q_rot.reshape(B, Qn, A_KV, A_HPG, A_HD).astype(f32) * scale
    t = q_pos
    NC = kc.shape[1]
    c_start = jnp.arange(NC) * CMP_STRIDE
    c_mask = (c_start + CMP_LEN - 1)[None, :] <= t[:, None]
    p_c = masked_softmax(jnp.einsum('bqghd,bngd->bghqn', qc, kc), c_mask)
    o_c = jnp.einsum('bghqn,bngd->bqghd', p_c, vc)
    TK = ks.shape[1]
    NS = -(-TK // SEL_BLK)

    def blocks(a):
        a = jnp.pad(a, ((0, 0), (0, NS * SEL_BLK - TK), (0, 0), (0, 0)))
        return jnp.transpose(a.reshape(B, NS, SEL_BLK, A_KV, A_HD), (0, 3, 1, 2, 4))

    ks_b, vs_b = blocks(ks), blocks(vs)
    s_start = jnp.arange(NS) * SEL_BLK
    overlap = ((c_start[:, None] < s_start[None, :] + SEL_BLK) &
               (c_start[:, None] + CMP_LEN > s_start[None, :])).astype(f32)
    imp = jnp.einsum('bghqn,ns->bgqs', p_c, overlap)
    cur = (t // SEL_BLK)[:, None]
    j = jnp.arange(NS)[None, :]
    valid = s_start[None, :] <= t[:, None]
    forced = (j == 0) | (j == cur) | (j == cur - 1)
    score = jnp.where(valid, jnp.where(forced, FORCE_SCORE, imp), NEG_INF)
    n_top = min(SEL_TOP, NS)
    _, idx = lax.top_k(score, n_top)
    sel_ok = jnp.take_along_axis(jnp.broadcast_to(valid, score.shape), idx, axis=-1)
    bi = jnp.arange(B)[:, None, None, None]
    gi = jnp.arange(A_KV)[None, :, None, None]
    k_sel = ks_b[bi, gi, idx]
    v_sel = vs_b[bi, gi, idx].reshape(B, A_KV, Qn, n_top * SEL_BLK, A_HD)
    key_pos = idx[..., None] * SEL_BLK + jnp.arange(SEL_BLK)
    s_mask = (key_pos <= t[None, None, :, None, None]) & sel_ok[..., None]
    s_s = jnp.einsum('bqghd,bgqnkd->bghqnk', qs, k_sel).reshape(B, A_KV, A_HPG, Qn, n_top * SEL_BLK)
    p_s = masked_softmax(s_s, s_mask.reshape(B, A_KV, 1, Qn, n_top * SEL_BLK))
    o_s = jnp.einsum('bghqm,bgqmd->bqghd', p_s, v_sel)
    w_mask = ((kw_pos[None, :] <= t[:, None]) & (kw_pos[None, :] > t[:, None] - WINDOW) & (kw_pos[None, :] >= 0))
    p_w = masked_softmax(jnp.einsum('bqghd,bkgd->bghqk', qs, kw), w_mask)
    o_w = jnp.einsum('bghqk,bkgd->bqghd', p_w, vw)
    g = gate.reshape(B, Qn, A_KV, A_HPG, 3)
    o = g[..., 0:1] * o_c + g[..., 1:2] * o_s + g[..., 2:3] * o_w
    return o.reshape(B, Qn, A_HEADS * A_HD)


def nsa_prompt(x, w_in, pe, w1, w2, w_out):
    B, T, _ = x.shape
    pos = jnp.arange(T, dtype=jnp.int32)
    qd, kvd = A_HEADS * A_HD, 6 * A_KV * A_HD
    q, kv, gate_logits = jnp.split(mm(x, w_in), [qd, qd + kvd], axis=-1)
    kv = kv.reshape(B, T, 6, A_KV, A_HD)
    rows = jnp.stack([kv[:, :, 0], kv[:, :, 1], partial_rope(kv[:, :, 2], pos), kv[:, :, 3]], axis=2)
    win = jnp.stack([partial_rope(kv[:, :, 4], pos), kv[:, :, 5]], axis=2)
    kc, vc = nsa_compress(rows[:, :, 0:2], pe, w1, w2)
    o = nsa_attend_prompt(q, rows.reshape(B, T, 4 * A_KV * A_HD), win.reshape(B, T, 2 * A_KV * A_HD), kc, vc,
                          gate_logits)
    return mm(o, w_out), rows, win[:, T - min(WINDOW, T):]


def nsa_sample(x, cache_kv, page_table, win_buf, w_in, pe, w1, w2, w_out):
    B, S, _ = x.shape
    past = page_table.shape[1] * PAGE_SIZE
    pos = past + jnp.arange(S, dtype=jnp.int32)
    q, q_rot, rows, win, gate = nsa_project(x, pos, w_in)
    past_rows = cache_kv[page_table].reshape(B, past, 4, A_KV, A_HD)
    all_rows = jnp.concatenate([past_rows.astype(rows.dtype), rows], axis=1)
    kc, vc = nsa_compress(all_rows[:, :, 0:2], pe, w1, w2)
    wb_len = win_buf.shape[1]
    kw_all = jnp.concatenate([win_buf.astype(win.dtype), win], axis=1)
    kw_pos = past - wb_len + jnp.arange(wb_len + S, dtype=jnp.int32)
    o = nsa_attend(q, q_rot, pos, kc, vc, all_rows[:, :, 2], all_rows[:, :, 3],
                   kw_all[:, :, 0], kw_all[:, :, 1], kw_pos, gate)
    return mm(o, w_out), rows, kw_all[:, S:]


def kernel(x_prompt, x_sample, state_mlstm_C, state_mlstm_n, state_mlstm_m, state_rwkv_S, state_rwkv_shift,
           state_conv, cache_nsa_kv, state_nsa_win, page_table, norm_g, mlp_up, mlp_down,
           m_w_in, m_b_gate, m_norm, m_w_out, r_mix, r_vec, r_w_rkv, r_w1, r_w2, r_a1, r_a2, r_g1, r_g2, r_wo,
           c_w_in, c_conv, c_w_out, a_w_in, a_cmp_pe, a_cmp_w1, a_cmp_w2, a_w_out):
    xp, xs = x_prompt, x_sample
    Bp = xp.shape[0]
    dt = xp.dtype
    depth = norm_g.shape[0]
    bf16 = jnp.bfloat16

    def h_dtype(layer):
        return dt if layer % N_MIXERS == 1 else bf16

    hp, hs = prenorm(xp, norm_g[0, 0], h_dtype(0)), prenorm(xs, norm_g[0, 0], h_dtype(0))
    for i in range(depth):
        kind = i % N_MIXERS
        if kind == 0:
            yp, (mC_p, mn_p, mm_p) = mlstm_prompt(hp, jnp.zeros((Bp, M_HEADS, M_DQK, M_DV), dt),
                                                  jnp.zeros((Bp, M_HEADS, M_DQK), dt), jnp.zeros((Bp, M_HEADS), dt),
                                                  m_w_in, m_b_gate, m_norm, m_w_out)
            ys, (mC_s, mn_s, mm_s) = mlstm_mixer(hs, state_mlstm_C, state_mlstm_n, state_mlstm_m,
                                                 m_w_in, m_b_gate, m_norm, m_w_out)
        elif kind == 1:
            yp, (rS_p, rx_p) = rwkv7_mixer(hp, jnp.zeros((Bp, R_HEADS, R_HEAD, R_HEAD), dt), jnp.zeros((Bp, D_MODEL), dt),
                                           r_mix, r_vec, r_w_rkv, r_w1, r_w2, r_a1, r_a2, r_g1, r_g2, r_wo, True)
            ys, (rS_s, rx_s) = rwkv7_mixer(hs, state_rwkv_S, state_rwkv_shift,
                                           r_mix, r_vec, r_w_rkv, r_w1, r_w2, r_a1, r_a2, r_g1, r_g2, r_wo, False)
        elif kind == 2:
            yp, cv_p = shortconv_mixer(hp, jnp.zeros((Bp, CONV_W - 1, D_MODEL), dt), c_w_in, c_conv, c_w_out)
            ys, cv_s = shortconv_mixer(hs, state_conv, c_w_in, c_conv, c_w_out)
        else:
            yp, kv_p, win_p = nsa_prompt(hp, a_w_in, a_cmp_pe, a_cmp_w1, a_cmp_w2, a_w_out)
            ys, kv_s, win_s = nsa_sample(hs, cache_nsa_kv, page_table, state_nsa_win,
                                         a_w_in, a_cmp_pe, a_cmp_w1, a_cmp_w2, a_w_out)
        xp, hp = residual_norm(xp, yp, norm_g[i, 1], norm_g[i, 2], bf16)
        xs, hs = residual_norm(xs, ys, norm_g[i, 1], norm_g[i, 2], bf16)
        g_next = norm_g[i + 1, 0] if i + 1 < depth else None
        xp, hp = residual_norm(xp, sq_relu_mlp(hp, mlp_up[i], mlp_down[i]), norm_g[i, 3], g_next, h_dtype(i + 1))
        xs, hs = residual_norm(xs, sq_relu_mlp(hs, mlp_up[i], mlp_down[i]), norm_g[i, 3], g_next, h_dtype(i + 1))
    return (xp, xs, mC_p, mC_s, mn_p, mn_s, mm_p, mm_s, rS_p, rS_s, rx_p, rx_s, cv_p, cv_s, kv_p, kv_s, win_p, win_s)
```

```python
import functools
import math

import jax
import jax.numpy as jnp
from jax import lax
from jax.experimental import pallas as pl
from jax.experimental.pallas import tpu as pltpu

D_MODEL = 4096
N_MIXERS = 4
NORM_EPS = 1e-6
NEG_INF = -1e30
TINY = 1e-30

M_HEADS = 8
M_DV = D_MODEL // M_HEADS
M_DQK = M_DV // 2
M_CHUNK = 64
M_GATE_CAP = 15.0

R_HEAD = 64
R_HEADS = D_MODEL // R_HEAD
R_GN_EPS = 64e-5

CONV_W = 3

A_HEADS = 32
A_HD = D_MODEL // A_HEADS
A_KV = 4
A_HPG = A_HEADS // A_KV
ROPE_DIM = A_HD // 4
ROPE_THETA = 500000.0
CMP_LEN = 32
CMP_STRIDE = 16
SEL_BLK = 64
SEL_TOP = 16
WINDOW = 512
NSA_QBLK = 64
FORCE_SCORE = 1e9
PAGE_SIZE = 128

VMEM_LIMIT_BYTES = 52 * 1024 * 1024


def _epilogue(acc, epilogue):
    if epilogue == 'relu2':
        r = jnp.maximum(acc, 0.0)
        return r * r
    return acc


def _mm_fullk_kernel(x_ref, w_ref, o_ref, *, epilogue):
    acc = jnp.dot(x_ref[...], w_ref[...].astype(jnp.bfloat16), preferred_element_type=jnp.float32)
    o_ref[...] = _epilogue(acc, epilogue).astype(o_ref.dtype)


def _mm_splitk_kernel(x_ref, w_ref, o_ref):
    @pl.when(pl.program_id(2) == 0)
    def _():
        o_ref[...] = jnp.zeros_like(o_ref)

    o_ref[...] += jnp.dot(x_ref[...], w_ref[...].astype(jnp.bfloat16), preferred_element_type=jnp.float32)


def _pick(n, pref):
    for t in pref:
        if n % t == 0:
            return t
    return n


MM_FULLK_MAX = 4096


def mm(x, w, epilogue=None, out_dtype=jnp.float32, layer=0):
    lead = x.shape[:-1]
    K = x.shape[-1]
    if w.ndim == 2:
        w = w[None]
    N = w.shape[2]
    x2 = x.reshape(-1, K).astype(jnp.bfloat16)
    M = x2.shape[0]
    tm = _pick(M, (2048, 1024, 512, 256, 128, 64, 32, 16))
    if K <= MM_FULLK_MAX:
        tn = 512 if N > 512 else N
        out = pl.pallas_call(
            functools.partial(_mm_fullk_kernel, epilogue=epilogue),
            grid=(M // tm, pl.cdiv(N, tn)),
            in_specs=[pl.BlockSpec((tm, K), lambda i, j: (i, 0), pipeline_mode=pl.Buffered(1)),
                      pl.BlockSpec((None, K, tn), lambda i, j: (layer, 0, j))],
            out_specs=pl.BlockSpec((tm, tn), lambda i, j: (i, j)),
            out_shape=jax.ShapeDtypeStruct((M, N), out_dtype),
            compiler_params=pltpu.CompilerParams(
                dimension_semantics=("parallel", "parallel"),
                vmem_limit_bytes=VMEM_LIMIT_BYTES),
            name="mm_fullk",
        )(x2, w)
    else:
        assert epilogue is None and out_dtype == jnp.float32
        tn = _pick(N, (1024, 512, 256, 128))
        tk = _pick(K, (1024, 512, 256, 128))
        out = pl.pallas_call(
            _mm_splitk_kernel,
            grid=(M // tm, N // tn, K // tk),
            in_specs=[pl.BlockSpec((tm, tk), lambda i, j, k: (i, k)),
                      pl.BlockSpec((None, tk, tn), lambda i, j, k: (layer, k, j))],
            out_specs=pl.BlockSpec((tm, tn), lambda i, j, k: (i, j)),
            out_shape=jax.ShapeDtypeStruct((M, N), jnp.float32),
            compiler_params=pltpu.CompilerParams(
                dimension_semantics=("parallel", "parallel", "arbitrary"),
                vmem_limit_bytes=VMEM_LIMIT_BYTES),
            name="mm_splitk",
        )(x2, w)
    return out.reshape(*lead, N)


NORM_ROWS = 256


def _rms(x, g):
    return x * lax.rsqrt(jnp.mean(x * x, axis=-1, keepdims=True) + NORM_EPS) * g


def _prenorm_kernel(x_ref, g_ref, h_ref):
    h_ref[...] = _rms(x_ref[...], g_ref[...]).astype(h_ref.dtype)


def _resnorm_kernel(x_ref, y_ref, gp_ref, gn_ref, xo_ref, h_ref):
    x = x_ref[...] + _rms(y_ref[...], gp_ref[...])
    xo_ref[...] = x
    h_ref[...] = _rms(x, gn_ref[...]).astype(h_ref.dtype)


def _res_kernel(x_ref, y_ref, gp_ref, xo_ref):
    xo_ref[...] = x_ref[...] + _rms(y_ref[...], gp_ref[...])


def _rowwise_call(body, arrays, gains, out_dtypes, name):
    lead = arrays[0].shape[:-1]
    D = arrays[0].shape[-1]
    arrays = [a.reshape(-1, D) for a in arrays]
    M = arrays[0].shape[0]
    tr = _pick(M, (NORM_ROWS,))
    row_spec = pl.BlockSpec((tr, D), lambda i: (i, 0))
    g_spec = pl.BlockSpec((1, D), lambda i: (0, 0))
    outs = pl.pallas_call(
        body,
        grid=(M // tr,),
        in_specs=[row_spec] * len(arrays) + [g_spec] * len(gains),
        out_specs=[row_spec] * len(out_dtypes),
        out_shape=[jax.ShapeDtypeStruct((M, D), dt) for dt in out_dtypes],
        compiler_params=pltpu.CompilerParams(dimension_semantics=("parallel",), vmem_limit_bytes=VMEM_LIMIT_BYTES),
        name=name,
    )(*arrays, *[g.reshape(1, D).astype(jnp.float32) for g in gains])
    return [o.reshape(*lead, D) for o in outs]


def prenorm(x, g, h_dtype):
    return _rowwise_call(_prenorm_kernel, [x], [g], [h_dtype], "prenorm")[0]


def residual_norm(x, y, g_post, g_next, h_dtype):
    if g_next is None:
        return _rowwise_call(_res_kernel, [x, y], [g_post], [jnp.float32], "residual")[0], None
    return _rowwise_call(_resnorm_kernel, [x, y], [g_post, g_next], [jnp.float32, h_dtype], "residual_norm")


def sq_relu_mlp(x, w_up, w_down, layer):
    h = mm(x, w_up, epilogue='relu2', out_dtype=jnp.bfloat16, layer=layer)
    return mm(h, w_down, layer=layer)


def masked_softmax(s, mask):
    s = jnp.where(mask, s.astype(jnp.float32), NEG_INF)
    m = jnp.max(s, axis=-1, keepdims=True)
    e = jnp.where(mask, jnp.exp(s - m), 0.0)
    return e / jnp.maximum(jnp.sum(e, axis=-1, keepdims=True), TINY)


def partial_rope(x, pos):
    half = ROPE_DIM // 2
    inv = ROPE_THETA ** (-jnp.arange(half, dtype=jnp.float32) / half)
    ang = pos.astype(jnp.float32)[:, None] * inv[None, :]
    cos, sin = jnp.cos(ang)[:, None, :], jnp.sin(ang)[:, None, :]
    xr = x[..., :ROPE_DIM].astype(jnp.float32)
    x1, x2 = xr[..., :half], xr[..., half:]
    rot = jnp.concatenate([x1 * cos - x2 * sin, x2 * cos + x1 * sin], axis=-1).astype(x.dtype)
    return jnp.concatenate([rot, x[..., ROPE_DIM:]], axis=-1)


def _softcap(z):
    return M_GATE_CAP * jnp.tanh(z / M_GATE_CAP)


def mlstm_mixer(x, C0, n0, m0, w_in, b_gate, g_norm, w_out):
    B, T, _ = x.shape
    f32 = jnp.float32
    qd, vd = M_HEADS * M_DQK, M_HEADS * M_DV
    q, k, v, o, ig, fg = jnp.split(mm(x, w_in), [qd, 2 * qd, 2 * qd + vd, 2 * qd + 2 * vd, 2 * qd + 2 * vd + M_HEADS], axis=-1)
    q = q.reshape(B, T, M_HEADS, M_DQK).astype(f32)
    k = k.reshape(B, T, M_HEADS, M_DQK).astype(f32) * (M_DQK ** -0.5)
    v = v.reshape(B, T, M_HEADS, M_DV).astype(f32)
    log_i = _softcap((ig + b_gate[0]).astype(f32))
    log_f = jax.nn.log_sigmoid(_softcap((fg + b_gate[1]).astype(f32)))
    L = math.gcd(T, M_CHUNK)
    nc = T // L

    def chunks(a):
        a = a.reshape(B, nc, L, M_HEADS, *a.shape[3:])
        return jnp.moveaxis(a, (1, 3), (0, 2))

    causal = jnp.tril(jnp.ones((L, L), dtype=bool))

    def step(carry, inp):
        C, n, m = carry
        qc, kc, vc, li, lf = inp
        b = jnp.cumsum(lf, axis=-1)
        dlog = jnp.where(causal, b[..., :, None] - b[..., None, :] + li[..., None, :], -jnp.inf)
        m_inter = b + m[..., None]
        m_s = jnp.maximum(m_inter, jnp.max(dlog, axis=-1))
        a = jnp.exp(dlog - m_s[..., None]) * jnp.einsum('bhsd,bhrd->bhsr', qc, kc)
        w_inter = jnp.exp(m_inter - m_s)
        num = w_inter[..., None] * jnp.einsum('bhsd,bhde->bhse', qc, C) + jnp.einsum('bhsr,bhre->bhse', a, vc)
        den = w_inter * jnp.einsum('bhsd,bhd->bhs', qc, n) + jnp.sum(a, axis=-1)
        h = num / jnp.maximum(jnp.abs(den), jnp.exp(-m_s))[..., None]
        b_last = b[..., -1]
        g_r = b_last[..., None] - b + li
        m_new = jnp.maximum(b_last + m, jnp.max(g_r, axis=-1))
        w_r = jnp.exp(g_r - m_new[..., None])
        w_c = jnp.exp(b_last + m - m_new)
        C_new = w_c[..., None, None] * C + jnp.einsum('bhr,bhrd,bhre->bhde', w_r, kc, vc)
        n_new = w_c[..., None] * n + jnp.einsum('bhr,bhrd->bhd', w_r, kc)
        return (C_new, n_new, m_new), h

    (C, n, m), h = lax.scan(step, (C0.astype(f32), n0.astype(f32), m0.astype(f32)),
                            (chunks(q), chunks(k), chunks(v), chunks(log_i), chunks(log_f)))
    h = jnp.moveaxis(h, (0, 2), (1, 3)).reshape(B, T, M_HEADS, M_DV)
    h = h * lax.rsqrt(jnp.mean(h * h, axis=-1, keepdims=True) + NORM_EPS) * g_norm
    h = h * jax.nn.sigmoid(o.reshape(B, T, M_HEADS, M_DV).astype(f32))
    y = mm(h.reshape(B, T, M_HEADS * M_DV), w_out)
    return y, (C.astype(C0.dtype), n.astype(n0.dtype), m.astype(m0.dtype))


def _mlstm_chunk_kernel(q_ref, k_ref, v_ref, o_ref, igc_ref, fgc_ref, igr_ref, fgr_ref, gn_ref, c0_ref, n0_ref, m0_ref,
                        y_ref, cout_ref, nout_ref, mout_ref, c_scr, n_scr, m_scr, *, nchunks):
    f32, bf16 = jnp.float32, jnp.bfloat16
    c = pl.program_id(2)

    @pl.when(c == 0)
    def _():
        c_scr[...] = c0_ref[0, 0]
        n_scr[...] = n0_ref[0, 0]
        m_scr[...] = m0_ref[0, 0]

    def softcap(z):
        return M_GATE_CAP * jnp.tanh(z / M_GATE_CAP)

    def log_sigmoid(z):
        return jnp.minimum(z, 0.0) - jnp.log(1.0 + jnp.exp(-jnp.abs(z)))

    L = q_ref.shape[0]
    q = q_ref[...]
    k = k_ref[...] * (M_DQK ** -0.5)
    v = v_ref[...].astype(bf16)
    li_c, li_r = softcap(igc_ref[0, 0, 0]), softcap(igr_ref[0, 0, 0])
    lf_c, lf_r = log_sigmoid(softcap(fgc_ref[0, 0, 0])), log_sigmoid(softcap(fgr_ref[0, 0, 0]))
    s_idx = lax.broadcasted_iota(jnp.int32, (L, L), 0)
    r_idx = lax.broadcasted_iota(jnp.int32, (L, L), 1)
    causal = r_idx <= s_idx
    b_c = jnp.sum(jnp.where(causal, lf_r, 0.0), axis=1, keepdims=True)
    b_r = jnp.sum(jnp.where(s_idx <= r_idx, lf_c, 0.0), axis=0, keepdims=True)
    m = m_scr[...]
    dlog = jnp.where(causal, b_c - b_r + li_r, -jnp.inf)
    m_inter = b_c + m
    m_s = jnp.maximum(m_inter, jnp.max(dlog, axis=1, keepdims=True))
    q16 = q.astype(bf16)
    a = jnp.exp(dlog - m_s) * _dot_nt(q16, k.astype(bf16))
    w_inter = jnp.exp(m_inter - m_s)
    num = (w_inter * jnp.dot(q16, c_scr[...].astype(bf16), preferred_element_type=f32)
           + jnp.dot(a.astype(bf16), v, preferred_element_type=f32))
    den = w_inter * jnp.sum(q * n_scr[...], axis=1, keepdims=True) + jnp.sum(a, axis=1, keepdims=True)
    h = num / jnp.maximum(jnp.abs(den), jnp.exp(-m_s))
    b_last = b_r[:, L - 1:L]
    m_new = jnp.maximum(b_last + m, jnp.max(b_last - b_r + li_r, axis=1, keepdims=True))
    w_c = jnp.exp(b_last + m - m_new)
    kw = k * jnp.exp(b_last - b_c + li_c - m_new)
    c_scr[...] = w_c * c_scr[...] + jnp.dot(kw.T.astype(bf16), v, preferred_element_type=f32)
    n_scr[...] = w_c * n_scr[...] + jnp.sum(kw, axis=0, keepdims=True)
    m_scr[...] = m_new

    hn = h * lax.rsqrt(jnp.mean(h * h, axis=1, keepdims=True) + NORM_EPS) * gn_ref[0]
    y_ref[...] = (hn * (1.0 / (1.0 + jnp.exp(-o_ref[...])))).astype(y_ref.dtype)

    @pl.when(c == nchunks - 1)
    def _():
        cout_ref[0, 0] = c_scr[...]
        nout_ref[0, 0] = n_scr[...]
        mout_ref[0, 0] = m_scr[...]


def mlstm_chunked(proj, b_gate, g_norm, C0, n0, m0, B, T):
    f32 = jnp.float32
    L = math.gcd(T, M_CHUNK)
    nc = T // L
    assert L % 16 == 0
    gates = proj[:, 2 * M_HEADS * M_DQK + 2 * M_HEADS * M_DV:].reshape(B, nc, L, 2, M_HEADS) + b_gate
    g_col = jnp.transpose(gates, (3, 0, 4, 1, 2))[..., None]
    g_row = jnp.transpose(gates, (3, 0, 4, 1, 2))[..., None, :]
    qk_blk = M_HEADS * M_DQK // M_DQK
    v_blk = 2 * M_HEADS * M_DQK // M_DV
    col_spec = pl.BlockSpec((1, 1, 1, L, 1), lambda b, h, c: (b, h, c, 0, 0))
    row_spec = pl.BlockSpec((1, 1, 1, 1, L), lambda b, h, c: (b, h, c, 0, 0))
    c_spec = pl.BlockSpec((1, 1, M_DQK, M_DV), lambda b, h, c: (b, h, 0, 0))
    n_spec = pl.BlockSpec((1, 1, 1, M_DQK), lambda b, h, c: (b, h, 0, 0))
    m_spec = pl.BlockSpec((1, 1, 1, 1), lambda b, h, c: (b, h, 0, 0))
    y, C, n, m = pl.pallas_call(
        functools.partial(_mlstm_chunk_kernel, nchunks=nc),
        grid=(B, M_HEADS, nc),
        in_specs=[pl.BlockSpec((L, M_DQK), lambda b, h, c: (b * nc + c, h)),
                  pl.BlockSpec((L, M_DQK), lambda b, h, c: (b * nc + c, qk_blk + h)),
                  pl.BlockSpec((L, M_DV), lambda b, h, c: (b * nc + c, v_blk + h)),
                  pl.BlockSpec((L, M_DV), lambda b, h, c: (b * nc + c, v_blk + M_HEADS + h)),
                  col_spec, col_spec, row_spec, row_spec,
                  pl.BlockSpec((1, 1, M_DV), lambda b, h, c: (h, 0, 0)),
                  c_spec, n_spec, m_spec],
        out_specs=[pl.BlockSpec((L, M_DV), lambda b, h, c: (b * nc + c, h)), c_spec, n_spec, m_spec],
        out_shape=[jax.ShapeDtypeStruct((B * T, M_HEADS * M_DV), jnp.bfloat16),
                   jax.ShapeDtypeStruct((B, M_HEADS, M_DQK, M_DV), f32),
                   jax.ShapeDtypeStruct((B, M_HEADS, 1, M_DQK), f32),
                   jax.ShapeDtypeStruct((B, M_HEADS, 1, 1), f32)],
        scratch_shapes=[pltpu.VMEM((M_DQK, M_DV), f32), pltpu.VMEM((1, M_DQK), f32), pltpu.VMEM((1, 1), f32)],
        compiler_params=pltpu.CompilerParams(
            dimension_semantics=("parallel", "parallel", "arbitrary"),
            vmem_limit_bytes=VMEM_LIMIT_BYTES),
        name="mlstm_chunk",
    )(proj, proj, proj, proj, g_col[0], g_col[1], g_row[0], g_row[1], g_norm.reshape(M_HEADS, 1, M_DV).astype(f32),
      C0.astype(f32), n0.astype(f32).reshape(B, M_HEADS, 1, M_DQK), m0.astype(f32).reshape(B, M_HEADS, 1, 1))
    return y, C, n.reshape(B, M_HEADS, M_DQK), m.reshape(B, M_HEADS)


def mlstm_prompt(x, C0, n0, m0, w_in, b_gate, g_norm, w_out):
    B, T, D = x.shape
    proj = mm(x.reshape(B * T, D), w_in)
    h, C, n, m = mlstm_chunked(proj, b_gate, g_norm, C0, n0, m0, B, T)
    return mm(h, w_out).reshape(B, T, D), (C.astype(C0.dtype), n.astype(n0.dtype), m.astype(m0.dtype))


RWKV_TC = 32
RWKV_ROWS = 8
RWKV_LANES = 128


def _rwkv_scan_kernel(r_ref, k_ref, v_ref, wp_ref, ap_ref, par_ref, s0_ref, y_ref, sout_ref,
                      s_scr, w_scr, b_scr, kx_scr, nkk_scr, y_scr, *, tc, nchunks, rows):
    c = pl.program_id(1)

    @pl.when(c == 0)
    def _():
        s_scr[...] = s0_ref[0]

    def prep(t, carry):
        kt = k_ref[0, t]
        z = -wp_ref[0, t]
        softplus = jnp.maximum(z, 0.0) + jnp.log(1.0 + jnp.exp(-jnp.abs(z)))
        w_scr[t] = jnp.exp(-jnp.exp(-softplus - 0.5))
        a = 1.0 / (1.0 + jnp.exp(-ap_ref[0, t]))
        kk = kt * par_ref[0]
        kk = kk / jnp.maximum(jnp.sqrt(jnp.sum(kk * kk, axis=0, keepdims=True)), 1e-12)
        b_scr[t] = kk * a
        nkk_scr[t] = -kk
        kx_scr[t] = kt * (1.0 + (a - 1.0) * par_ref[1])
        return carry

    lax.fori_loop(0, tc, prep, 0)

    def vblock(vb, carry):
        v0 = vb * rows

        def step(t, inner):
            for i in range(rows):
                S = s_scr[v0 + i]
                sa = jnp.sum(S * nkk_scr[t], axis=0, keepdims=True)
                vrow = v_ref[0, t, pl.ds(v0 + i, 1), :]
                S = S * w_scr[t] + sa * b_scr[t] + vrow * kx_scr[t]
                s_scr[v0 + i] = S
                y_scr[t, pl.ds(v0 + i, 1), :] = jnp.sum(S * r_ref[0, t], axis=0, keepdims=True)
            return inner

        lax.fori_loop(0, tc, step, 0)
        return carry

    lax.fori_loop(0, R_HEAD // rows, vblock, 0)

    def post(t, carry):
        y = y_scr[t]
        mu = jnp.mean(y, axis=0, keepdims=True)
        var = jnp.mean((y - mu) ** 2, axis=0, keepdims=True)
        yn = ((y - mu) * lax.rsqrt(var + R_GN_EPS)) * par_ref[3] + par_ref[4]
        bonus = jnp.sum(r_ref[0, t] * kx_scr[t] * par_ref[2], axis=0, keepdims=True) * v_ref[0, t]
        y_ref[0, t] = yn + bonus
        return carry

    lax.fori_loop(0, tc, post, 0)

    @pl.when(c == nchunks - 1)
    def _():
        sout_ref[0] = s_scr[...]


def _to_chain(z, B, T):
    z = z.reshape(B // 2, 2, T, R_HEADS, R_HEAD)
    return jnp.transpose(z, (0, 2, 4, 1, 3)).reshape(B // 2, T, R_HEAD, RWKV_LANES)


def _vec_to_chain(vec):
    return jnp.tile(vec.reshape(R_HEADS, R_HEAD).T, (1, 2))


def rwkv_scan(r, k, v, w_pre, a_pre, S0, k_k, k_a, r_k, gn_w, gn_b):
    B, T, D = r.shape
    assert B % 2 == 0 and 2 * R_HEADS == RWKV_LANES and T % RWKV_TC == 0
    G = B // 2
    nchunks = T // RWKV_TC
    f32 = jnp.float32
    ins = [_to_chain(z.astype(f32), B, T) for z in (r, k, v, w_pre, a_pre)]
    par = jnp.stack([_vec_to_chain(p.astype(f32)) for p in (k_k, k_a, r_k, gn_w, gn_b)])
    s0 = jnp.transpose(S0.astype(f32).reshape(G, 2, R_HEADS, R_HEAD, R_HEAD), (0, 3, 4, 1, 2))
    s0 = s0.reshape(G, R_HEAD, R_HEAD, RWKV_LANES)
    seq_spec = pl.BlockSpec((1, RWKV_TC, R_HEAD, RWKV_LANES), lambda g, c: (g, c, 0, 0))
    st_spec = pl.BlockSpec((1, R_HEAD, R_HEAD, RWKV_LANES), lambda g, c: (g, 0, 0, 0))
    seq_scr = pltpu.VMEM((RWKV_TC, R_HEAD, RWKV_LANES), f32)
    y, s_out = pl.pallas_call(
        functools.partial(_rwkv_scan_kernel, tc=RWKV_TC, nchunks=nchunks, rows=RWKV_ROWS),
        grid=(G, nchunks),
        in_specs=[seq_spec] * 5 + [pl.BlockSpec((5, R_HEAD, RWKV_LANES), lambda g, c: (0, 0, 0)), st_spec],
        out_specs=[seq_spec, st_spec],
        out_shape=[jax.ShapeDtypeStruct((G, T, R_HEAD, RWKV_LANES), f32),
                   jax.ShapeDtypeStruct((G, R_HEAD, R_HEAD, RWKV_LANES), f32)],
        scratch_shapes=[pltpu.VMEM((R_HEAD, R_HEAD, RWKV_LANES), f32)] + [seq_scr] * 5,
        compiler_params=pltpu.CompilerParams(
            dimension_semantics=("parallel", "arbitrary"),
            vmem_limit_bytes=VMEM_LIMIT_BYTES),
        name="rwkv_scan",
    )(*ins, par, s0)
    y = jnp.transpose(y.reshape(G, T, R_HEAD, 2, R_HEADS), (0, 3, 1, 4, 2)).reshape(B, T, D)
    s_out = jnp.transpose(s_out.reshape(G, R_HEAD, R_HEAD, 2, R_HEADS), (0, 3, 4, 1, 2))
    return y, s_out.reshape(B, R_HEADS, R_HEAD, R_HEAD)


def rwkv7_mixer(x, S0, shift0, mix, vec, w_rkv, w1, w2, a1, a2, g1, g2, wo, use_scan_kernel):
    B, T, D = x.shape
    f32 = jnp.float32
    bf16 = jnp.bfloat16
    xx = jnp.concatenate([shift0[:, None, :].astype(x.dtype), x[:, :-1]], axis=1) - x

    def lerp(j):
        return (x + xx * mix[j]).astype(bf16)

    w0, a0, k_k, k_a, r_k, gn_w, gn_b = vec[0], vec[1], vec[2], vec[3], vec[4], vec[5], vec[6]
    r = mm(lerp(0), w_rkv, layer=0)
    k = mm(lerp(2), w_rkv, layer=1)
    v = mm(lerp(3), w_rkv, layer=2)
    w_pre = (w0 + mm(jnp.tanh(mm(lerp(1), w1)).astype(bf16), w2)).astype(f32)
    a_pre = (a0 + mm(mm(lerp(4), a1).astype(bf16), a2)).astype(f32)
    g = mm(jax.nn.sigmoid(mm(lerp(5), g1)).astype(bf16), g2)
    if use_scan_kernel:
        y, S = rwkv_scan(r, k, v, w_pre, a_pre, S0, k_k, k_a, r_k, gn_w, gn_b)
        return mm((y * g).astype(bf16), wo), (S.astype(S0.dtype), x[:, -1])

    w_log = -jax.nn.softplus(-w_pre) - 0.5
    decay = jnp.exp(-jnp.exp(w_log))
    a = jax.nn.sigmoid(a_pre)

    def heads(z):
        return z.reshape(B, T, R_HEADS, R_HEAD).astype(f32)

    kk = heads(k * k_k)
    kk = kk / jnp.maximum(jnp.sqrt(jnp.sum(kk * kk, axis=-1, keepdims=True)), 1e-12)
    k = k.astype(f32) * (1.0 + (a - 1.0) * k_a)
    r_h, k_h, v_h, a_h, w_h = heads(r), heads(k), heads(v), heads(a), heads(decay)

    def step(S, inp):
        r_t, w_t, k_t, v_t, kk_t, a_t = inp
        sa = jnp.sum(S * (-kk_t)[:, :, None, :], axis=-1)
        S = S * w_t[:, :, None, :] + sa[..., None] * (kk_t * a_t)[:, :, None, :] + v_t[..., None] * k_t[:, :, None, :]
        return S, jnp.sum(S * r_t[:, :, None, :], axis=-1)

    S, y = lax.scan(step, S0.astype(f32),
                    (jnp.moveaxis(r_h, 1, 0), jnp.moveaxis(w_h, 1, 0), jnp.moveaxis(k_h, 1, 0),
                     jnp.moveaxis(v_h, 1, 0), jnp.moveaxis(kk, 1, 0), jnp.moveaxis(a_h, 1, 0)))
    y = jnp.moveaxis(y, 0, 1)
    mu = jnp.mean(y, axis=-1, keepdims=True)
    var = jnp.mean((y - mu) ** 2, axis=-1, keepdims=True)
    y = ((y - mu) * lax.rsqrt(var + R_GN_EPS)).reshape(B, T, D) * gn_w + gn_b
    bonus = jnp.sum(r_h * k_h * r_k.reshape(R_HEADS, R_HEAD), axis=-1, keepdims=True) * v_h
    y = (y + bonus.reshape(B, T, D)) * g
    return mm(y.astype(bf16), wo), (S.astype(S0.dtype), x[:, -1])


def shortconv_mixer(x, buf, w_in, conv_w, w_out):
    T = x.shape[1]
    b_gate, c_gate, u = jnp.split(mm(x, w_in), 3, axis=-1)
    z = jnp.concatenate([buf.astype(u.dtype), c_gate * u], axis=1)
    conv = sum(z[:, j:j + T] * conv_w[j] for j in range(CONV_W))
    return mm(b_gate * conv, w_out), z[:, T:]


NSA_QB = 128
NSA_KT = 512
NSA_WT = WINDOW + NSA_QB
NSA_SLOTS = 128


def _dot_nt(a, b):
    return lax.dot_general(a, b, (((1,), (1,)), ((), ())), preferred_element_type=jnp.float32)


def _softmax_rows(s, mask):
    s = jnp.where(mask, s, NEG_INF)
    m = jnp.max(s, axis=-1, keepdims=True)
    e = jnp.where(mask, jnp.exp(s - m), 0.0)
    return e / jnp.maximum(jnp.sum(e, axis=-1, keepdims=True), TINY)


def _nsa_attn_kernel(q_ref, ks_ref, vs_ref, kw_ref, vw_ref, kc_ref, vc_ref, gate_ref, cos_ref, sna_ref, snb_ref,
                     ovl_ref, exp_ref, o_ref, *, qb, seq, n_cmp, n_sel):
    f32, bf16 = jnp.float32, jnp.bfloat16
    t0 = pl.program_id(2) * qb
    scale = A_HD ** -0.5
    cos, sna, snb = cos_ref[...], sna_ref[...], snb_ref[...]

    def rope(xh):
        return xh * cos + pltpu.roll(xh, A_HD - ROPE_DIM // 2, 1) * sna + pltpu.roll(xh, ROPE_DIM // 2, 1) * snb

    heads = [q_ref[0, :, h * A_HD:(h + 1) * A_HD] for h in range(A_HPG)]
    qc = jnp.concatenate([xh * scale for xh in heads], axis=0)
    qs = jnp.concatenate([(rope(xh) * scale).astype(bf16) for xh in heads], axis=0)
    rows = A_HPG * qb
    tq = t0 + lax.broadcasted_iota(jnp.int32, (qb, 1), 0)
    tq3 = tq[None]

    lane = lax.broadcasted_iota(jnp.int32, (1, 1, NSA_SLOTS), 2)
    kc = kc_ref[0]
    qc_hi, kc_hi = qc.astype(bf16), kc.astype(bf16)
    qc_lo, kc_lo = (qc - qc_hi.astype(f32)).astype(bf16), (kc - kc_hi.astype(f32)).astype(bf16)
    s_c = (_dot_nt(qc_hi, kc_hi) + _dot_nt(qc_hi, kc_lo) + _dot_nt(qc_lo, kc_hi)).reshape(A_HPG, qb, NSA_SLOTS)
    c_mask = (lane * CMP_STRIDE + (CMP_LEN - 1) <= tq3) & (lane < n_cmp)
    p_c = _softmax_rows(s_c, c_mask)
    o_c = jnp.dot(p_c.reshape(rows, NSA_SLOTS).astype(bf16), vc_ref[0].astype(bf16), preferred_element_type=f32)
    psum = jnp.sum(p_c, axis=0)
    ovl = ovl_ref[...]
    hi = psum.astype(bf16)
    r1 = psum - hi.astype(f32)
    mid = r1.astype(bf16)
    lo = (r1 - mid.astype(f32)).astype(bf16)
    imp = (jnp.dot(hi, ovl, preferred_element_type=f32) + jnp.dot(mid, ovl, preferred_element_type=f32)
           + jnp.dot(lo, ovl, preferred_element_type=f32))

    sidx = lax.broadcasted_iota(jnp.int32, (qb, NSA_SLOTS), 1)
    cur = lax.shift_right_logical(tq, SEL_BLK.bit_length() - 1)
    valid = (sidx * SEL_BLK <= tq) & (sidx < n_sel)
    forced = (sidx == 0) | (sidx == cur) | (sidx == cur - 1)
    score = jnp.where(valid, jnp.where(forced, FORCE_SCORE, imp), NEG_INF)
    rank = jnp.zeros((qb, NSA_SLOTS), jnp.int32)
    for j in range(n_sel):
        cj = score[:, j:j + 1]
        beats = (cj > score) | ((cj == score) & (sidx > j))
        rank = rank + beats.astype(jnp.int32)
    sel = ((rank < SEL_TOP) & valid).astype(bf16)

    kpos0 = lax.broadcasted_iota(jnp.int32, (qb, NSA_KT), 1)

    def sel_tile(kt, carry):
        m, l, acc = carry
        k0 = pl.multiple_of(kt * NSA_KT, NSA_KT)
        s = _dot_nt(qs, ks_ref[0, pl.ds(k0, NSA_KT), :].astype(bf16)).reshape(A_HPG, qb, NSA_KT)
        picked = jnp.dot(sel, exp_ref[kt], preferred_element_type=f32) > 0.5
        mask = (picked & (kpos0 + k0 <= tq))[None]
        s = jnp.where(mask, s, NEG_INF)
        m_new = jnp.maximum(m, jnp.max(s, axis=-1, keepdims=True))
        alpha = jnp.exp(m - m_new)
        e = jnp.where(mask, jnp.exp(s - m_new), 0.0)
        l = alpha * l + jnp.sum(e, axis=-1, keepdims=True)
        pv = jnp.dot(e.reshape(rows, NSA_KT).astype(bf16), vs_ref[0, pl.ds(k0, NSA_KT), :].astype(bf16),
                     preferred_element_type=f32)
        acc = alpha * acc + pv.reshape(A_HPG, qb, A_HD)
        return m_new, l, acc

    n_tiles = (t0 + qb + NSA_KT - 1) // NSA_KT
    m0 = jnp.full((A_HPG, qb, 1), NEG_INF, f32)
    _, l_s, acc_s = lax.fori_loop(0, n_tiles, sel_tile,
                                  (m0, jnp.zeros((A_HPG, qb, 1), f32), jnp.zeros((A_HPG, qb, A_HD), f32)))
    o_s = acc_s / jnp.maximum(l_s, TINY)

    w0 = pl.multiple_of(jnp.clip(t0 + qb - NSA_WT, 0, seq - NSA_WT), qb)
    s_w = _dot_nt(qs, kw_ref[0, pl.ds(w0, NSA_WT), :].astype(bf16)).reshape(A_HPG, qb, NSA_WT)
    wpos = w0 + lax.broadcasted_iota(jnp.int32, (1, 1, NSA_WT), 2)
    w_mask = (wpos <= tq3) & (wpos > tq3 - WINDOW)
    p_w = _softmax_rows(s_w, w_mask)
    o_w = jnp.dot(p_w.reshape(rows, NSA_WT).astype(bf16), vw_ref[0, pl.ds(w0, NSA_WT), :].astype(bf16),
                  preferred_element_type=f32).reshape(A_HPG, qb, A_HD)

    o_c = o_c.reshape(A_HPG, qb, A_HD)
    gate = 1.0 / (1.0 + jnp.exp(-gate_ref[0, 0]))
    for h in range(A_HPG):
        o_ref[0, :, h * A_HD:(h + 1) * A_HD] = (gate[:, 3 * h:3 * h + 1] * o_c[h]
                                                + gate[:, 3 * h + 1:3 * h + 2] * o_s[h]
                                                + gate[:, 3 * h + 2:3 * h + 3] * o_w[h])


def nsa_attend_prompt(q, rows, win, kc, vc, gate_logits):
    B, T, _ = q.shape
    f32, bf16 = jnp.float32, jnp.bfloat16
    assert T % NSA_QB == 0 and T % NSA_KT == 0 and T >= NSA_WT and T % SEL_BLK == 0
    n_cmp = kc.shape[1]
    n_sel = T // SEL_BLK
    assert n_cmp <= NSA_SLOTS and n_sel <= NSA_SLOTS
    pad = ((0, 0), (0, NSA_SLOTS - n_cmp), (0, 0), (0, 0))
    kc_p = jnp.pad(kc, pad).reshape(B, NSA_SLOTS, A_KV * A_HD)
    vc_p = jnp.pad(vc, pad).reshape(B, NSA_SLOTS, A_KV * A_HD)
    gate_g = jnp.transpose(gate_logits.reshape(B, T, A_KV, 3 * A_HPG), (0, 2, 1, 3)).astype(f32)
    half = ROPE_DIM // 2
    inv = ROPE_THETA ** (-jnp.arange(half, dtype=f32) / half)
    ang = jnp.arange(T, dtype=jnp.int32).astype(f32)[:, None] * inv[None, :]
    zeros = jnp.zeros((T, A_HD - ROPE_DIM), f32)
    cos_t = jnp.concatenate([jnp.cos(ang), jnp.cos(ang), 1.0 + zeros], axis=1)
    sna_t = jnp.concatenate([-jnp.sin(ang), jnp.zeros((T, half), f32), zeros], axis=1)
    snb_t = jnp.concatenate([jnp.zeros((T, half), f32), jnp.sin(ang), zeros], axis=1)
    c_start = jnp.arange(NSA_SLOTS) * CMP_STRIDE
    s_start = jnp.arange(NSA_SLOTS) * SEL_BLK
    ovl = ((c_start[:, None] < s_start[None, :] + SEL_BLK) & (c_start[:, None] + CMP_LEN > s_start[None, :])
           & (jnp.arange(NSA_SLOTS)[:, None] < n_cmp) & (jnp.arange(NSA_SLOTS)[None, :] < n_sel)).astype(bf16)
    expand = (jnp.arange(NSA_SLOTS)[:, None] == (jnp.arange(T)[None, :] // SEL_BLK)).astype(bf16)
    expand = jnp.transpose(expand.reshape(NSA_SLOTS, T // NSA_KT, NSA_KT), (1, 0, 2))

    def kv_spec(col0):
        return pl.BlockSpec((1, T, A_HD), lambda b, g, i: (b, 0, col0 + g))

    tab_spec = pl.BlockSpec((NSA_QB, A_HD), lambda b, g, i: (i, 0))
    return pl.pallas_call(
        functools.partial(_nsa_attn_kernel, qb=NSA_QB, seq=T, n_cmp=n_cmp, n_sel=n_sel),
        grid=(B, A_KV, T // NSA_QB),
        in_specs=[pl.BlockSpec((1, NSA_QB, A_HPG * A_HD), lambda b, g, i: (b, i, g)),
                  kv_spec(2 * A_KV), kv_spec(3 * A_KV), kv_spec(0), kv_spec(A_KV),
                  pl.BlockSpec((1, NSA_SLOTS, A_HD), lambda b, g, i: (b, 0, g)),
                  pl.BlockSpec((1, NSA_SLOTS, A_HD), lambda b, g, i: (b, 0, g)),
                  pl.BlockSpec((1, 1, NSA_QB, 3 * A_HPG), lambda b, g, i: (b, g, i, 0)),
                  tab_spec, tab_spec, tab_spec,
                  pl.BlockSpec((NSA_SLOTS, NSA_SLOTS), lambda b, g, i: (0, 0)),
                  pl.BlockSpec((T // NSA_KT, NSA_SLOTS, NSA_KT), lambda b, g, i: (0, 0, 0))],
        out_specs=pl.BlockSpec((1, NSA_QB, A_HPG * A_HD), lambda b, g, i: (b, i, g)),
        out_shape=jax.ShapeDtypeStruct((B, T, A_HEADS * A_HD), f32),
        compiler_params=pltpu.CompilerParams(
            dimension_semantics=("parallel", "parallel", "arbitrary"),
            vmem_limit_bytes=VMEM_LIMIT_BYTES),
        name="nsa_attn_prompt",
    )(q, rows, rows, win, win, kc_p, vc_p, gate_g, cos_t, sna_t, snb_t, ovl, expand)


def nsa_project(x, pos, w_in):
    B, T, _ = x.shape
    qd, kvd = A_HEADS * A_HD, 6 * A_KV * A_HD
    q, kv, gate = jnp.split(mm(x, w_in), [qd, qd + kvd], axis=-1)
    q = q.reshape(B, T, A_HEADS, A_HD)
    kv = kv.reshape(B, T, 6, A_KV, A_HD)
    rows = jnp.stack([kv[:, :, 0], kv[:, :, 1], partial_rope(kv[:, :, 2], pos), kv[:, :, 3]], axis=2)
    win = jnp.stack([partial_rope(kv[:, :, 4], pos), kv[:, :, 5]], axis=2)
    gate = jax.nn.sigmoid(gate.reshape(B, T, A_HEADS, 3).astype(jnp.float32))
    return q, partial_rope(q, pos), rows, win, gate


def nsa_compress(kv_cmp, pe, w1, w2):
    B, T = kv_cmp.shape[:2]
    nc = (T - CMP_LEN) // CMP_STRIDE + 1
    idx = (jnp.arange(nc) * CMP_STRIDE)[:, None] + jnp.arange(CMP_LEN)[None, :]
    blk = kv_cmp[:, idx] + jnp.transpose(pe, (1, 0, 2))[:, :, None, :]
    flat = jnp.transpose(blk, (0, 1, 3, 4, 2, 5)).reshape(B, nc, 2, A_KV, CMP_LEN * A_HD)
    hid = jax.nn.gelu(jnp.einsum('bnckf,cfm->bnckm', flat, w1))
    out = jnp.einsum('bnckm,cmd->bnckd', hid, w2)
    return out[:, :, 0], out[:, :, 1]


def nsa_attend(q, q_rot, q_pos, kc, vc, ks, vs, kw, vw, kw_pos, gate):
    B, Qn = q.shape[:2]
    f32 = jnp.float32
    scale = A_HD ** -0.5
    qc = q.reshape(B, Qn, A_KV, A_HPG, A_HD).astype(f32) * scale
    qs = q_rot.reshape(B, Qn, A_KV, A_HPG, A_HD).astype(f32) * scale
    t = q_pos
    NC = kc.shape[1]
    c_start = jnp.arange(NC) * CMP_STRIDE
    c_mask = (c_start + CMP_LEN - 1)[None, :] <= t[:, None]
    p_c = masked_softmax(jnp.einsum('bqghd,bngd->bghqn', qc, kc), c_mask)
    o_c = jnp.einsum('bghqn,bngd->bqghd', p_c, vc)
    TK = ks.shape[1]
    NS = -(-TK // SEL_BLK)

    def blocks(a):
        a = jnp.pad(a, ((0, 0), (0, NS * SEL_BLK - TK), (0, 0), (0, 0)))
        return jnp.transpose(a.reshape(B, NS, SEL_BLK, A_KV, A_HD), (0, 3, 1, 2, 4))

    ks_b, vs_b = blocks(ks), blocks(vs)
    s_start = jnp.arange(NS) * SEL_BLK
    overlap = ((c_start[:, None] < s_start[None, :] + SEL_BLK) &
               (c_start[:, None] + CMP_LEN > s_start[None, :])).astype(f32)
    imp = jnp.einsum('bghqn,ns->bgqs', p_c, overlap)
    cur = (t // SEL_BLK)[:, None]
    j = jnp.arange(NS)[None, :]
    valid = s_start[None, :] <= t[:, None]
    forced = (j == 0) | (j == cur) | (j == cur - 1)
    score = jnp.where(valid, jnp.where(forced, FORCE_SCORE, imp), NEG_INF)
    n_top = min(SEL_TOP, NS)
    _, idx = lax.top_k(score, n_top)
    sel_ok = jnp.take_along_axis(jnp.broadcast_to(valid, score.shape), idx, axis=-1)
    bi = jnp.arange(B)[:, None, None, None]
    gi = jnp.arange(A_KV)[None, :, None, None]
    k_sel = ks_b[bi, gi, idx]
    v_sel = vs_b[bi, gi, idx].reshape(B, A_KV, Qn, n_top * SEL_BLK, A_HD)
    key_pos = idx[..., None] * SEL_BLK + jnp.arange(SEL_BLK)
    s_mask = (key_pos <= t[None, None, :, None, None]) & sel_ok[..., None]
    s_s = jnp.einsum('bqghd,bgqnkd->bghqnk', qs, k_sel).reshape(B, A_KV, A_HPG, Qn, n_top * SEL_BLK)
    p_s = masked_softmax(s_s, s_mask.reshape(B, A_KV, 1, Qn, n_top * SEL_BLK))
    o_s = jnp.einsum('bghqm,bgqmd->bqghd', p_s, v_sel)
    w_mask = ((kw_pos[None, :] <= t[:, None]) & (kw_pos[None, :] > t[:, None] - WINDOW) & (kw_pos[None, :] >= 0))
    p_w = masked_softmax(jnp.einsum('bqghd,bkgd->bghqk', qs, kw), w_mask)
    o_w = jnp.einsum('bghqk,bkgd->bqghd', p_w, vw)
    g = gate.reshape(B, Qn, A_KV, A_HPG, 3)
    o = g[..., 0:1] * o_c + g[..., 1:2] * o_s + g[..., 2:3] * o_w
    return o.reshape(B, Qn, A_HEADS * A_HD)


def nsa_prompt(x, w_in, pe, w1, w2, w_out):
    B, T, _ = x.shape
    pos = jnp.arange(T, dtype=jnp.int32)
    qd, kvd = A_HEADS * A_HD, 6 * A_KV * A_HD
    q, kv, gate_logits = jnp.split(mm(x, w_in), [qd, qd + kvd], axis=-1)
    kv = kv.reshape(B, T, 6, A_KV, A_HD)
    rows = jnp.stack([kv[:, :, 0], kv[:, :, 1], partial_rope(kv[:, :, 2], pos), kv[:, :, 3]], axis=2)
    win = jnp.stack([partial_rope(kv[:, :, 4], pos), kv[:, :, 5]], axis=2)
    kc, vc = nsa_compress(rows[:, :, 0:2], pe, w1, w2)
    o = nsa_attend_prompt(q, rows.reshape(B, T, 4 * A_KV * A_HD), win.reshape(B, T, 2 * A_KV * A_HD), kc, vc,
                          gate_logits)
    return mm(o, w_out), rows, win[:, T - min(WINDOW, T):]


CMP_PER_PAGE = PAGE_SIZE // CMP_STRIDE
ROW_W = 4 * A_KV * A_HD


def _nsa_cmp_paged_kernel(pt_ref, pa_ref, pb_ref, perm_ref, pe_ref, w1_ref, w2_ref, o_ref, ys_scr, *, npairs,
                          rows_per_g):
    f32, bf16 = jnp.float32, jnp.bfloat16
    p2 = pl.program_id(1)
    xa = jnp.dot(perm_ref[...], pa_ref[0].astype(bf16), preferred_element_type=f32)
    xb = jnp.dot(perm_ref[...], pb_ref[0].astype(bf16), preferred_element_type=f32)
    for c in range(2):
        for g in range(A_KV):
            col = (c * A_KV + g) * A_HD
            row0 = pl.multiple_of(g * rows_per_g + p2 * (2 * CMP_PER_PAGE), 2 * CMP_PER_PAGE)
            for l in range(CMP_STRIDE):
                r0 = l * CMP_PER_PAGE
                ys_scr[c, pl.ds(row0, 2 * CMP_PER_PAGE), l * A_HD:(l + 1) * A_HD] = jnp.concatenate(
                    [xa[r0:r0 + CMP_PER_PAGE, col:col + A_HD], xb[r0:r0 + CMP_PER_PAGE, col:col + A_HD]],
                    axis=0).astype(bf16)

    @pl.when(p2 == npairs - 1)
    def _():
        half = CMP_STRIDE * A_HD
        for c in range(2):
            w1 = w1_ref[c].astype(bf16)
            ys = ys_scr[c]
            first = jnp.dot(ys, w1[:half], preferred_element_type=f32)
            second = jnp.dot(ys, w1[half:], preferred_element_type=f32)
            pe_rows = jnp.broadcast_to(pe_ref[c], (8, CMP_LEN * A_HD)).astype(bf16)
            bias = jnp.dot(pe_rows, w1, preferred_element_type=f32)[0:1]
            hid = jax.nn.gelu(first + pltpu.roll(second, ys.shape[0] - 1, 0) + bias)
            o_ref[0, c] = jnp.dot(hid.astype(bf16), w2_ref[c].astype(bf16), preferred_element_type=f32)


def nsa_compress_paged(cache2, page_table, pe, w1, w2):
    B, P = page_table.shape
    assert P % 2 == 0
    rows_per_g = P * CMP_PER_PAGE
    half_w = ROW_W // 2
    src = jnp.arange(PAGE_SIZE)
    perm = (jnp.arange(PAGE_SIZE)[None, :]
            == ((src % CMP_PER_PAGE) * CMP_STRIDE + src // CMP_PER_PAGE)[:, None]).astype(jnp.bfloat16)

    def page_spec(which):
        return pl.BlockSpec((1, PAGE_SIZE, half_w), lambda b, p, pt: (pt[b * P + 2 * p + which], 0, 0))

    grid_spec = pltpu.PrefetchScalarGridSpec(
        num_scalar_prefetch=1,
        grid=(B, P // 2),
        in_specs=[page_spec(0), page_spec(1),
                  pl.BlockSpec((PAGE_SIZE, PAGE_SIZE), lambda b, p, pt: (0, 0)),
                  pl.BlockSpec((2, 1, CMP_LEN * A_HD), lambda b, p, pt: (0, 0, 0)),
                  pl.BlockSpec(w1.shape, lambda b, p, pt: (0, 0, 0), pipeline_mode=pl.Buffered(1)),
                  pl.BlockSpec(w2.shape, lambda b, p, pt: (0, 0, 0))],
        out_specs=pl.BlockSpec((1, 2, A_KV * rows_per_g, A_HD), lambda b, p, pt: (b, 0, 0, 0)),
        scratch_shapes=[pltpu.VMEM((2, A_KV * rows_per_g, CMP_STRIDE * A_HD), jnp.bfloat16)])
    return pl.pallas_call(
        functools.partial(_nsa_cmp_paged_kernel, npairs=P // 2, rows_per_g=rows_per_g),
        grid_spec=grid_spec,
        out_shape=jax.ShapeDtypeStruct((B, 2, A_KV * rows_per_g, A_HD), jnp.float32),
        compiler_params=pltpu.CompilerParams(
            dimension_semantics=("parallel", "arbitrary"),
            vmem_limit_bytes=VMEM_LIMIT_BYTES),
        name="nsa_compress_paged",
    )(page_table.reshape(-1).astype(jnp.int32), cache2, cache2, perm, pe.reshape(2, 1, CMP_LEN * A_HD), w1, w2)


def _nsa_decode_kernel(pt_ref, qc_ref, qs_ref, gate_ref, new_ref, kcv_ref, win_ref, ovl_ref, exp_ref, pa_ref, pb_ref,
                       o_ref, sel_scr, oc_scr, ow_scr, m_scr, l_scr, acc_scr, *, npairs, n_cmp, n_sel, past, slots):
    f32, bf16 = jnp.float32, jnp.bfloat16
    p2 = pl.program_id(1)
    rows_per_g = kcv_ref.shape[2] // A_KV
    wlen = win_ref.shape[1]

    @pl.when(p2 == 0)
    def _():
        lane = lax.broadcasted_iota(jnp.int32, (1, rows_per_g), 1)
        c_mask = (lane * CMP_STRIDE + (CMP_LEN - 1) <= past) & (lane < n_cmp)
        sidx = lax.broadcasted_iota(jnp.int32, (1, slots), 1)
        jidx = lax.broadcasted_iota(jnp.int32, (slots, slots), 0)
        cur = past // SEL_BLK
        valid = (sidx * SEL_BLK <= past) & (sidx < n_sel)
        forced = (sidx == 0) | (sidx == cur) | (sidx == cur - 1)
        wpos = past - wlen + lax.broadcasted_iota(jnp.int32, (1, wlen), 1)
        w_mask = (wpos <= past) & (wpos > past - WINDOW) & (wpos >= 0)
        for g in range(A_KV):
            qc = qc_ref[0, g]
            kc = kcv_ref[0, 0, g * rows_per_g:(g + 1) * rows_per_g, :]
            vc = kcv_ref[0, 1, g * rows_per_g:(g + 1) * rows_per_g, :]
            qc_hi, kc_hi = qc.astype(bf16), kc.astype(bf16)
            qc_lo, kc_lo = (qc - qc_hi.astype(f32)).astype(bf16), (kc - kc_hi.astype(f32)).astype(bf16)
            s_c = _dot_nt(qc_hi, kc_hi) + _dot_nt(qc_hi, kc_lo) + _dot_nt(qc_lo, kc_hi)
            p_c = _softmax_rows(s_c, c_mask)
            oc_scr[g] = jnp.dot(p_c.astype(bf16), vc.astype(bf16), preferred_element_type=f32)
            psum = jnp.broadcast_to(jnp.sum(p_c, axis=0, keepdims=True), (8, rows_per_g))
            hi = psum.astype(bf16)
            r1 = psum - hi.astype(f32)
            mid = r1.astype(bf16)
            lo = (r1 - mid.astype(f32)).astype(bf16)
            ovl = ovl_ref[...]
            imp = (jnp.dot(hi, ovl, preferred_element_type=f32) + jnp.dot(mid, ovl, preferred_element_type=f32)
                   + jnp.dot(lo, ovl, preferred_element_type=f32))
            score = jnp.where(valid, jnp.where(forced, FORCE_SCORE, imp), NEG_INF)
            col = score.T[:, 0:1]
            row = score[0:1]
            beats = (col > row) | ((col == row) & (jidx < sidx))
            rank = jnp.sum(beats.astype(jnp.int32), axis=0, keepdims=True)
            sel_scr[g] = jnp.broadcast_to(((rank < SEL_TOP) & valid).astype(f32), (8, slots))
            qs = qs_ref[0, g]
            kw = win_ref[0, :, g * A_HD:(g + 1) * A_HD]
            vw = win_ref[0, :, (A_KV + g) * A_HD:(A_KV + g + 1) * A_HD]
            s_w = jnp.where(w_mask, _dot_nt(qs.astype(bf16), kw.astype(bf16)), NEG_INF)
            s_n = jnp.sum(qs.astype(bf16).astype(f32) * new_ref[0, 2, g].astype(bf16).astype(f32), axis=1, keepdims=True)
            m_w = jnp.maximum(jnp.max(s_w, axis=1, keepdims=True), s_n)
            e_w = jnp.where(w_mask, jnp.exp(s_w - m_w), 0.0)
            e_n = jnp.exp(s_n - m_w)
            den = jnp.maximum(jnp.sum(e_w, axis=1, keepdims=True) + e_n, TINY)
            ow_scr[g] = (jnp.dot((e_w / den).astype(bf16), vw.astype(bf16), preferred_element_type=f32)
                         + (e_n / den).astype(bf16).astype(f32) * new_ref[0, 3, g].astype(bf16).astype(f32))
        m_scr[...] = jnp.full(m_scr.shape, NEG_INF, f32)
        l_scr[...] = jnp.zeros(l_scr.shape, f32)
        acc_scr[...] = jnp.zeros(acc_scr.shape, f32)

    for g in range(A_KV):
        qs = qs_ref[0, g].astype(bf16)
        k = jnp.concatenate([pa_ref[0, :, g * A_HD:(g + 1) * A_HD], pb_ref[0, :, g * A_HD:(g + 1) * A_HD]], axis=0)
        v = jnp.concatenate([pa_ref[0, :, (A_KV + g) * A_HD:(A_KV + g + 1) * A_HD],
                             pb_ref[0, :, (A_KV + g) * A_HD:(A_KV + g + 1) * A_HD]], axis=0)
        picked = jnp.dot(sel_scr[g].astype(bf16), exp_ref[0], preferred_element_type=f32) > 0.5
        s = jnp.where(picked, _dot_nt(qs, k.astype(bf16)), NEG_INF)
        m_new = jnp.maximum(m_scr[g], jnp.max(s, axis=1, keepdims=True))
        alpha = jnp.exp(m_scr[g] - m_new)
        e = jnp.where(picked, jnp.exp(s - m_new), 0.0)
        l_scr[g] = alpha * l_scr[g] + jnp.sum(e, axis=1, keepdims=True)
        acc_scr[g] = alpha * acc_scr[g] + jnp.dot(e.astype(bf16), v.astype(bf16), preferred_element_type=f32)
        m_scr[g] = m_new

    @pl.when(p2 == npairs - 1)
    def _():
        last = n_sel - 1
        for g in range(A_KV):
            qs = qs_ref[0, g].astype(bf16).astype(f32)
            s_n = jnp.sum(qs * new_ref[0, 0, g].astype(bf16).astype(f32), axis=1, keepdims=True)
            ok = sel_scr[g][:, last:last + 1] > 0.5
            s_n = jnp.where(ok, s_n, NEG_INF)
            m_new = jnp.maximum(m_scr[g], s_n)
            alpha = jnp.exp(m_scr[g] - m_new)
            e_n = jnp.where(ok, jnp.exp(s_n - m_new), 0.0)
            l_fin = alpha * l_scr[g] + e_n
            acc = alpha * acc_scr[g] + e_n.astype(bf16).astype(f32) * new_ref[0, 1, g].astype(bf16).astype(f32)
            o_s = acc / jnp.maximum(l_fin, TINY)
            gt = gate_ref[0, g]
            o_ref[0, g] = gt[:, 0:1] * oc_scr[g] + gt[:, 1:2] * o_s + gt[:, 2:3] * ow_scr[g]


def nsa_decode_attend(q, q_rot, gate, rows_new, win_new, kcv, cache2, page_table, win_buf):
    B, P = page_table.shape
    f32, bf16 = jnp.float32, jnp.bfloat16
    past = P * PAGE_SIZE
    n_cmp = (past + 1 - CMP_LEN) // CMP_STRIDE + 1
    n_sel = -(-(past + 1) // SEL_BLK)
    rows_per_g = P * CMP_PER_PAGE
    slots = -(-n_sel // 128) * 128
    assert n_sel == past // SEL_BLK + 1 and n_cmp <= rows_per_g and P % 2 == 0
    scale = A_HD ** -0.5
    qc = (q.astype(f32) * scale).reshape(B, A_KV, A_HPG, A_HD)
    qs = (q_rot.astype(f32) * scale).reshape(B, A_KV, A_HPG, A_HD)
    gt = gate.reshape(B, A_KV, A_HPG, 3)
    new = jnp.stack([rows_new[:, 0, 2], rows_new[:, 0, 3], win_new[:, 0, 0], win_new[:, 0, 1]], axis=1)
    new = jnp.broadcast_to(new[:, :, :, None, :], (B, 4, A_KV, 1, A_HD))
    wlen = win_buf.shape[1]
    win2 = win_buf.reshape(B, wlen, 2 * A_KV * A_HD)
    c_start = jnp.arange(rows_per_g) * CMP_STRIDE
    s_start = jnp.arange(slots) * SEL_BLK
    ovl = ((c_start[:, None] < s_start[None, :] + SEL_BLK) & (c_start[:, None] + CMP_LEN > s_start[None, :])
           & (jnp.arange(rows_per_g)[:, None] < n_cmp) & (jnp.arange(slots)[None, :] < n_sel)).astype(bf16)
    keys = jnp.arange(past) // SEL_BLK
    expand = (jnp.arange(slots)[:, None] == keys[None, :]).astype(bf16)
    expand = jnp.transpose(expand.reshape(slots, P // 2, 2 * PAGE_SIZE), (1, 0, 2))
    half_w = ROW_W // 2

    def page_spec(which):
        return pl.BlockSpec((1, PAGE_SIZE, half_w), lambda b, p, pt: (pt[b * P + 2 * p + which], 0, 1))

    def per_b(shape):
        nd = len(shape)
        return pl.BlockSpec((1,) + tuple(shape[1:]), lambda b, p, pt: (b,) + (0,) * (nd - 1))

    grid_spec = pltpu.PrefetchScalarGridSpec(
        num_scalar_prefetch=1,
        grid=(B, P // 2),
        in_specs=[per_b(qc.shape), per_b(qs.shape), per_b(gt.shape), per_b(new.shape), per_b(kcv.shape),
                  per_b(win2.shape),
                  pl.BlockSpec(ovl.shape, lambda b, p, pt: (0, 0)),
                  pl.BlockSpec((1, slots, 2 * PAGE_SIZE), lambda b, p, pt: (p, 0, 0)),
                  page_spec(0), page_spec(1)],
        out_specs=pl.BlockSpec((1, A_KV, A_HPG, A_HD), lambda b, p, pt: (b, 0, 0, 0)),
        scratch_shapes=[pltpu.VMEM((A_KV, A_HPG, slots), f32), pltpu.VMEM((A_KV, A_HPG, A_HD), f32),
                        pltpu.VMEM((A_KV, A_HPG, A_HD), f32), pltpu.VMEM((A_KV, A_HPG, 1), f32),
                        pltpu.VMEM((A_KV, A_HPG, 1), f32), pltpu.VMEM((A_KV, A_HPG, A_HD), f32)])
    o = pl.pallas_call(
        functools.partial(_nsa_decode_kernel, npairs=P // 2, n_cmp=n_cmp, n_sel=n_sel, past=past, slots=slots),
        grid_spec=grid_spec,
        out_shape=jax.ShapeDtypeStruct((B, A_KV, A_HPG, A_HD), f32),
        compiler_params=pltpu.CompilerParams(
            dimension_semantics=("parallel", "arbitrary"),
            vmem_limit_bytes=VMEM_LIMIT_BYTES),
        name="nsa_decode",
    )(page_table.reshape(-1).astype(jnp.int32), qc, qs, gt, new, kcv, win2, ovl, expand, cache2, cache2)
    return o.reshape(B, 1, A_HEADS * A_HD)


def nsa_sample(x, cache_kv, page_table, win_buf, w_in, pe, w1, w2, w_out):
    B, S, _ = x.shape
    assert S == 1
    past = page_table.shape[1] * PAGE_SIZE
    pos = past + jnp.arange(S, dtype=jnp.int32)
    q, q_rot, rows, win, gate = nsa_project(x, pos, w_in)
    cache2 = cache_kv.reshape(cache_kv.shape[0], PAGE_SIZE, ROW_W)
    kcv = nsa_compress_paged(cache2, page_table, pe, w1, w2)
    o = nsa_decode_attend(q, q_rot, gate, rows, win, kcv, cache2, page_table, win_buf)
    kw_all = jnp.concatenate([win_buf.astype(win.dtype), win], axis=1)
    return mm(o, w_out), rows, kw_all[:, S:]


def kernel(x_prompt, x_sample, state_mlstm_C, state_mlstm_n, state_mlstm_m, state_rwkv_S, state_rwkv_shift,
           state_conv, cache_nsa_kv, state_nsa_win, page_table, norm_g, mlp_up, mlp_down,
           m_w_in, m_b_gate, m_norm, m_w_out, r_mix, r_vec, r_w_rkv, r_w1, r_w2, r_a1, r_a2, r_g1, r_g2, r_wo,
           c_w_in, c_conv, c_w_out, a_w_in, a_cmp_pe, a_cmp_w1, a_cmp_w2, a_w_out):
    xp, xs = x_prompt, x_sample
    Bp = xp.shape[0]
    dt = xp.dtype
    depth = norm_g.shape[0]
    bf16 = jnp.bfloat16

    def h_dtype(layer):
        return dt if layer % N_MIXERS == 1 else bf16

    hp, hs = prenorm(xp, norm_g[0, 0], h_dtype(0)), prenorm(xs, norm_g[0, 0], h_dtype(0))
    for i in range(depth):
        kind = i % N_MIXERS
        if kind == 0:
            yp, (mC_p, mn_p, mm_p) = mlstm_prompt(hp, jnp.zeros((Bp, M_HEADS, M_DQK, M_DV), dt),
                                                  jnp.zeros((Bp, M_HEADS, M_DQK), dt), jnp.zeros((Bp, M_HEADS), dt),
                                                  m_w_in, m_b_gate, m_norm, m_w_out)
            ys, (mC_s, mn_s, mm_s) = mlstm_mixer(hs, state_mlstm_C, state_mlstm_n, state_mlstm_m,
                                                 m_w_in, m_b_gate, m_norm, m_w_out)
        elif kind == 1:
            yp, (rS_p, rx_p) = rwkv7_mixer(hp, jnp.zeros((Bp, R_HEADS, R_HEAD, R_HEAD), dt), jnp.zeros((Bp, D_MODEL), dt),
                                           r_mix, r_vec, r_w_rkv, r_w1, r_w2, r_a1, r_a2, r_g1, r_g2, r_wo, True)
            ys, (rS_s, rx_s) = rwkv7_mixer(hs, state_rwkv_S, state_rwkv_shift,
                                           r_mix, r_vec, r_w_rkv, r_w1, r_w2, r_a1, r_a2, r_g1, r_g2, r_wo, False)
        elif kind == 2:
            yp, cv_p = shortconv_mixer(hp, jnp.zeros((Bp, CONV_W - 1, D_MODEL), dt), c_w_in, c_conv, c_w_out)
            ys, cv_s = shortconv_mixer(hs, state_conv, c_w_in, c_conv, c_w_out)
        else:
            yp, kv_p, win_p = nsa_prompt(hp, a_w_in, a_cmp_pe, a_cmp_w1, a_cmp_w2, a_w_out)
            ys, kv_s, win_s = nsa_sample(hs, cache_nsa_kv, page_table, state_nsa_win,
                                         a_w_in, a_cmp_pe, a_cmp_w1, a_cmp_w2, a_w_out)
        xp, hp = residual_norm(xp, yp, norm_g[i, 1], norm_g[i, 2], bf16)
        xs, hs = residual_norm(xs, ys, norm_g[i, 1], norm_g[i, 2], bf16)
        g_next = norm_g[i + 1, 0] if i + 1 < depth else None
        xp, hp = residual_norm(xp, sq_relu_mlp(hp, mlp_up, mlp_down, i), norm_g[i, 3], g_next, h_dtype(i + 1))
        xs, hs = residual_norm(xs, sq_relu_mlp(hs, mlp_up, mlp_down, i), norm_g[i, 3], g_next, h_dtype(i + 1))
    return (xp, xs, mC_p, mC_s, mn_p, mn_s, mm_p, mm_s, rS_p, rS_s, rx_p, rx_s, cv_p, cv_s, kv_p, kv_s, win_p, win_s)
```

```python
import functools
import math

import jax
import jax.numpy as jnp
from jax import lax
from jax.experimental import pallas as pl
from jax.experimental.pallas import tpu as pltpu

D_MODEL = 4096
N_MIXERS = 4
NORM_EPS = 1e-6
NEG_INF = -1e30
TINY = 1e-30

M_HEADS = 8
M_DV = D_MODEL // M_HEADS
M_DQK = M_DV // 2
M_CHUNK = 64
M_GATE_CAP = 15.0

R_HEAD = 64
R_HEADS = D_MODEL // R_HEAD
R_GN_EPS = 64e-5

CONV_W = 3

A_HEADS = 32
A_HD = D_MODEL // A_HEADS
A_KV = 4
A_HPG = A_HEADS // A_KV
ROPE_DIM = A_HD // 4
ROPE_THETA = 500000.0
CMP_LEN = 32
CMP_STRIDE = 16
SEL_BLK = 64
SEL_TOP = 16
WINDOW = 512
FORCE_SCORE = 1e9
PAGE_SIZE = 128

VMEM_LIMIT_BYTES = 52 * 1024 * 1024


def _epilogue(acc, epilogue):
    if epilogue == 'relu2':
        r = jnp.maximum(acc, 0.0)
        return r * r
    return acc


def _mm_fullk_kernel(x_ref, w_ref, o_ref, *, epilogue):
    acc = jnp.dot(x_ref[...], w_ref[...].astype(jnp.bfloat16), preferred_element_type=jnp.float32)
    o_ref[...] = _epilogue(acc, epilogue).astype(o_ref.dtype)


def _mm_splitk_kernel(x_ref, w_ref, o_ref):
    @pl.when(pl.program_id(2) == 0)
    def _():
        o_ref[...] = jnp.zeros_like(o_ref)

    o_ref[...] += jnp.dot(x_ref[...], w_ref[...].astype(jnp.bfloat16), preferred_element_type=jnp.float32)


def _pick(n, pref):
    for t in pref:
        if n % t == 0:
            return t
    return n


MM_FULLK_MAX = 4096


def mm(x, w, epilogue=None, out_dtype=jnp.float32, layer=0):
    lead = x.shape[:-1]
    K = x.shape[-1]
    if w.ndim == 2:
        w = w[None]
    N = w.shape[2]
    x2 = x.reshape(-1, K).astype(jnp.bfloat16)
    M = x2.shape[0]
    tm = _pick(M, (2048, 1024, 512, 256, 128, 64, 32, 16))
    if K <= MM_FULLK_MAX:
        tn = 512 if N > 512 else N
        out = pl.pallas_call(
            functools.partial(_mm_fullk_kernel, epilogue=epilogue),
            grid=(M // tm, pl.cdiv(N, tn)),
            in_specs=[pl.BlockSpec((tm, K), lambda i, j: (i, 0), pipeline_mode=pl.Buffered(1)),
                      pl.BlockSpec((None, K, tn), lambda i, j: (layer, 0, j))],
            out_specs=pl.BlockSpec((tm, tn), lambda i, j: (i, j)),
            out_shape=jax.ShapeDtypeStruct((M, N), out_dtype),
            compiler_params=pltpu.CompilerParams(
                dimension_semantics=("parallel", "parallel"),
                vmem_limit_bytes=VMEM_LIMIT_BYTES),
            name="mm_fullk",
        )(x2, w)
    else:
        assert epilogue is None and out_dtype == jnp.float32
        tn = _pick(N, (1024, 512, 256, 128))
        tk = _pick(K, (1024, 512, 256, 128))
        out = pl.pallas_call(
            _mm_splitk_kernel,
            grid=(M // tm, N // tn, K // tk),
            in_specs=[pl.BlockSpec((tm, tk), lambda i, j, k: (i, k)),
                      pl.BlockSpec((None, tk, tn), lambda i, j, k: (layer, k, j))],
            out_specs=pl.BlockSpec((tm, tn), lambda i, j, k: (i, j)),
            out_shape=jax.ShapeDtypeStruct((M, N), jnp.float32),
            compiler_params=pltpu.CompilerParams(
                dimension_semantics=("parallel", "parallel", "arbitrary"),
                vmem_limit_bytes=VMEM_LIMIT_BYTES),
            name="mm_splitk",
        )(x2, w)
    return out.reshape(*lead, N)


NORM_ROWS = 256


def _rms(x, g):
    return x * lax.rsqrt(jnp.mean(x * x, axis=-1, keepdims=True) + NORM_EPS) * g


def _prenorm_kernel(x_ref, g_ref, h_ref):
    h_ref[...] = _rms(x_ref[...], g_ref[...]).astype(h_ref.dtype)


def _resnorm_kernel(x_ref, y_ref, gp_ref, gn_ref, xo_ref, h_ref):
    x = x_ref[...] + _rms(y_ref[...], gp_ref[...])
    xo_ref[...] = x
    h_ref[...] = _rms(x, gn_ref[...]).astype(h_ref.dtype)


def _res_kernel(x_ref, y_ref, gp_ref, xo_ref):
    xo_ref[...] = x_ref[...] + _rms(y_ref[...], gp_ref[...])


def _rowwise_call(body, arrays, gains, out_dtypes, name):
    lead = arrays[0].shape[:-1]
    D = arrays[0].shape[-1]
    arrays = [a.reshape(-1, D) for a in arrays]
    M = arrays[0].shape[0]
    tr = _pick(M, (NORM_ROWS,))
    row_spec = pl.BlockSpec((tr, D), lambda i: (i, 0))
    g_spec = pl.BlockSpec((1, D), lambda i: (0, 0))
    outs = pl.pallas_call(
        body,
        grid=(M // tr,),
        in_specs=[row_spec] * len(arrays) + [g_spec] * len(gains),
        out_specs=[row_spec] * len(out_dtypes),
        out_shape=[jax.ShapeDtypeStruct((M, D), dt) for dt in out_dtypes],
        compiler_params=pltpu.CompilerParams(dimension_semantics=("parallel",), vmem_limit_bytes=VMEM_LIMIT_BYTES),
        name=name,
    )(*arrays, *[g.reshape(1, D).astype(jnp.float32) for g in gains])
    return [o.reshape(*lead, D) for o in outs]


def prenorm(x, g, h_dtype):
    return _rowwise_call(_prenorm_kernel, [x], [g], [h_dtype], "prenorm")[0]


def residual_norm(x, y, g_post, g_next, h_dtype):
    if g_next is None:
        return _rowwise_call(_res_kernel, [x, y], [g_post], [jnp.float32], "residual")[0], None
    return _rowwise_call(_resnorm_kernel, [x, y], [g_post, g_next], [jnp.float32, h_dtype], "residual_norm")


def sq_relu_mlp(x, w_up, w_down, layer):
    h = mm(x, w_up, epilogue='relu2', out_dtype=jnp.bfloat16, layer=layer)
    return mm(h, w_down, layer=layer)


def partial_rope(x, pos):
    half = ROPE_DIM // 2
    inv = ROPE_THETA ** (-jnp.arange(half, dtype=jnp.float32) / half)
    ang = pos.astype(jnp.float32)[:, None] * inv[None, :]
    cos, sin = jnp.cos(ang)[:, None, :], jnp.sin(ang)[:, None, :]
    xr = x[..., :ROPE_DIM].astype(jnp.float32)
    x1, x2 = xr[..., :half], xr[..., half:]
    rot = jnp.concatenate([x1 * cos - x2 * sin, x2 * cos + x1 * sin], axis=-1).astype(x.dtype)
    return jnp.concatenate([rot, x[..., ROPE_DIM:]], axis=-1)


def _softcap(z):
    return M_GATE_CAP * jnp.tanh(z / M_GATE_CAP)


def mlstm_mixer(x, C0, n0, m0, w_in, b_gate, g_norm, w_out):
    B, T, _ = x.shape
    f32 = jnp.float32
    qd, vd = M_HEADS * M_DQK, M_HEADS * M_DV
    q, k, v, o, ig, fg = jnp.split(mm(x, w_in), [qd, 2 * qd, 2 * qd + vd, 2 * qd + 2 * vd, 2 * qd + 2 * vd + M_HEADS], axis=-1)
    q = q.reshape(B, T, M_HEADS, M_DQK).astype(f32)
    k = k.reshape(B, T, M_HEADS, M_DQK).astype(f32) * (M_DQK ** -0.5)
    v = v.reshape(B, T, M_HEADS, M_DV).astype(f32)
    log_i = _softcap((ig + b_gate[0]).astype(f32))
    log_f = jax.nn.log_sigmoid(_softcap((fg + b_gate[1]).astype(f32)))
    L = math.gcd(T, M_CHUNK)
    nc = T // L

    def chunks(a):
        a = a.reshape(B, nc, L, M_HEADS, *a.shape[3:])
        return jnp.moveaxis(a, (1, 3), (0, 2))

    causal = jnp.tril(jnp.ones((L, L), dtype=bool))

    def step(carry, inp):
        C, n, m = carry
        qc, kc, vc, li, lf = inp
        b = jnp.cumsum(lf, axis=-1)
        dlog = jnp.where(causal, b[..., :, None] - b[..., None, :] + li[..., None, :], -jnp.inf)
        m_inter = b + m[..., None]
        m_s = jnp.maximum(m_inter, jnp.max(dlog, axis=-1))
        a = jnp.exp(dlog - m_s[..., None]) * jnp.einsum('bhsd,bhrd->bhsr', qc, kc)
        w_inter = jnp.exp(m_inter - m_s)
        num = w_inter[..., None] * jnp.einsum('bhsd,bhde->bhse', qc, C) + jnp.einsum('bhsr,bhre->bhse', a, vc)
        den = w_inter * jnp.einsum('bhsd,bhd->bhs', qc, n) + jnp.sum(a, axis=-1)
        h = num / jnp.maximum(jnp.abs(den), jnp.exp(-m_s))[..., None]
        b_last = b[..., -1]
        g_r = b_last[..., None] - b + li
        m_new = jnp.maximum(b_last + m, jnp.max(g_r, axis=-1))
        w_r = jnp.exp(g_r - m_new[..., None])
        w_c = jnp.exp(b_last + m - m_new)
        C_new = w_c[..., None, None] * C + jnp.einsum('bhr,bhrd,bhre->bhde', w_r, kc, vc)
        n_new = w_c[..., None] * n + jnp.einsum('bhr,bhrd->bhd', w_r, kc)
        return (C_new, n_new, m_new), h

    (C, n, m), h = lax.scan(step, (C0.astype(f32), n0.astype(f32), m0.astype(f32)),
                            (chunks(q), chunks(k), chunks(v), chunks(log_i), chunks(log_f)))
    h = jnp.moveaxis(h, (0, 2), (1, 3)).reshape(B, T, M_HEADS, M_DV)
    h = h * lax.rsqrt(jnp.mean(h * h, axis=-1, keepdims=True) + NORM_EPS) * g_norm
    h = h * jax.nn.sigmoid(o.reshape(B, T, M_HEADS, M_DV).astype(f32))
    y = mm(h.reshape(B, T, M_HEADS * M_DV), w_out)
    return y, (C.astype(C0.dtype), n.astype(n0.dtype), m.astype(m0.dtype))


def _mlstm_chunk_kernel(q_ref, k_ref, v_ref, o_ref, igc_ref, fgc_ref, igr_ref, fgr_ref, gn_ref, c0_ref, n0_ref, m0_ref,
                        y_ref, cout_ref, nout_ref, mout_ref, c_scr, n_scr, m_scr, *, nchunks):
    f32, bf16 = jnp.float32, jnp.bfloat16
    c = pl.program_id(2)

    @pl.when(c == 0)
    def _():
        c_scr[...] = c0_ref[0, 0]
        n_scr[...] = n0_ref[0, 0]
        m_scr[...] = m0_ref[0, 0]

    def softcap(z):
        return M_GATE_CAP * jnp.tanh(z / M_GATE_CAP)

    def log_sigmoid(z):
        return jnp.minimum(z, 0.0) - jnp.log(1.0 + jnp.exp(-jnp.abs(z)))

    L = q_ref.shape[0]
    q = q_ref[...]
    k = k_ref[...] * (M_DQK ** -0.5)
    v = v_ref[...].astype(bf16)
    li_c, li_r = softcap(igc_ref[0, 0, 0]), softcap(igr_ref[0, 0, 0])
    lf_c, lf_r = log_sigmoid(softcap(fgc_ref[0, 0, 0])), log_sigmoid(softcap(fgr_ref[0, 0, 0]))
    s_idx = lax.broadcasted_iota(jnp.int32, (L, L), 0)
    r_idx = lax.broadcasted_iota(jnp.int32, (L, L), 1)
    causal = r_idx <= s_idx
    b_c = jnp.sum(jnp.where(causal, lf_r, 0.0), axis=1, keepdims=True)
    b_r = jnp.sum(jnp.where(s_idx <= r_idx, lf_c, 0.0), axis=0, keepdims=True)
    m = m_scr[...]
    dlog = jnp.where(causal, b_c - b_r + li_r, -jnp.inf)
    m_inter = b_c + m
    m_s = jnp.maximum(m_inter, jnp.max(dlog, axis=1, keepdims=True))
    q16 = q.astype(bf16)
    a = jnp.exp(dlog - m_s) * _dot_nt(q16, k.astype(bf16))
    w_inter = jnp.exp(m_inter - m_s)
    num = (w_inter * jnp.dot(q16, c_scr[...].astype(bf16), preferred_element_type=f32)
           + jnp.dot(a.astype(bf16), v, preferred_element_type=f32))
    den = w_inter * jnp.sum(q * n_scr[...], axis=1, keepdims=True) + jnp.sum(a, axis=1, keepdims=True)
    h = num / jnp.maximum(jnp.abs(den), jnp.exp(-m_s))
    b_last = b_r[:, L - 1:L]
    m_new = jnp.maximum(b_last + m, jnp.max(b_last - b_r + li_r, axis=1, keepdims=True))
    w_c = jnp.exp(b_last + m - m_new)
    kw = k * jnp.exp(b_last - b_c + li_c - m_new)
    c_scr[...] = w_c * c_scr[...] + jnp.dot(kw.T.astype(bf16), v, preferred_element_type=f32)
    n_scr[...] = w_c * n_scr[...] + jnp.sum(kw, axis=0, keepdims=True)
    m_scr[...] = m_new

    hn = h * lax.rsqrt(jnp.mean(h * h, axis=1, keepdims=True) + NORM_EPS) * gn_ref[0]
    y_ref[...] = (hn * (1.0 / (1.0 + jnp.exp(-o_ref[...])))).astype(y_ref.dtype)

    @pl.when(c == nchunks - 1)
    def _():
        cout_ref[0, 0] = c_scr[...]
        nout_ref[0, 0] = n_scr[...]
        mout_ref[0, 0] = m_scr[...]


def mlstm_chunked(proj, b_gate, g_norm, C0, n0, m0, B, T):
    f32 = jnp.float32
    L = math.gcd(T, M_CHUNK)
    nc = T // L
    assert L % 16 == 0
    gates = proj[:, 2 * M_HEADS * M_DQK + 2 * M_HEADS * M_DV:].reshape(B, nc, L, 2, M_HEADS) + b_gate
    g_col = jnp.transpose(gates, (3, 0, 4, 1, 2))[..., None]
    g_row = jnp.transpose(gates, (3, 0, 4, 1, 2))[..., None, :]
    qk_blk = M_HEADS * M_DQK // M_DQK
    v_blk = 2 * M_HEADS * M_DQK // M_DV
    col_spec = pl.BlockSpec((1, 1, 1, L, 1), lambda b, h, c: (b, h, c, 0, 0))
    row_spec = pl.BlockSpec((1, 1, 1, 1, L), lambda b, h, c: (b, h, c, 0, 0))
    c_spec = pl.BlockSpec((1, 1, M_DQK, M_DV), lambda b, h, c: (b, h, 0, 0))
    n_spec = pl.BlockSpec((1, 1, 1, M_DQK), lambda b, h, c: (b, h, 0, 0))
    m_spec = pl.BlockSpec((1, 1, 1, 1), lambda b, h, c: (b, h, 0, 0))
    y, C, n, m = pl.pallas_call(
        functools.partial(_mlstm_chunk_kernel, nchunks=nc),
        grid=(B, M_HEADS, nc),
        in_specs=[pl.BlockSpec((L, M_DQK), lambda b, h, c: (b * nc + c, h)),
                  pl.BlockSpec((L, M_DQK), lambda b, h, c: (b * nc + c, qk_blk + h)),
                  pl.BlockSpec((L, M_DV), lambda b, h, c: (b * nc + c, v_blk + h)),
                  pl.BlockSpec((L, M_DV), lambda b, h, c: (b * nc + c, v_blk + M_HEADS + h)),
                  col_spec, col_spec, row_spec, row_spec,
                  pl.BlockSpec((1, 1, M_DV), lambda b, h, c: (h, 0, 0)),
                  c_spec, n_spec, m_spec],
        out_specs=[pl.BlockSpec((L, M_DV), lambda b, h, c: (b * nc + c, h)), c_spec, n_spec, m_spec],
        out_shape=[jax.ShapeDtypeStruct((B * T, M_HEADS * M_DV), jnp.bfloat16),
                   jax.ShapeDtypeStruct((B, M_HEADS, M_DQK, M_DV), f32),
                   jax.ShapeDtypeStruct((B, M_HEADS, 1, M_DQK), f32),
                   jax.ShapeDtypeStruct((B, M_HEADS, 1, 1), f32)],
        scratch_shapes=[pltpu.VMEM((M_DQK, M_DV), f32), pltpu.VMEM((1, M_DQK), f32), pltpu.VMEM((1, 1), f32)],
        compiler_params=pltpu.CompilerParams(
            dimension_semantics=("parallel", "parallel", "arbitrary"),
            vmem_limit_bytes=VMEM_LIMIT_BYTES),
        name="mlstm_chunk",
    )(proj, proj, proj, proj, g_col[0], g_col[1], g_row[0], g_row[1], g_norm.reshape(M_HEADS, 1, M_DV).astype(f32),
      C0.astype(f32), n0.astype(f32).reshape(B, M_HEADS, 1, M_DQK), m0.astype(f32).reshape(B, M_HEADS, 1, 1))
    return y, C, n.reshape(B, M_HEADS, M_DQK), m.reshape(B, M_HEADS)


def mlstm_prompt(x, C0, n0, m0, w_in, b_gate, g_norm, w_out):
    B, T, D = x.shape
    proj = mm(x.reshape(B * T, D), w_in)
    h, C, n, m = mlstm_chunked(proj, b_gate, g_norm, C0, n0, m0, B, T)
    return mm(h, w_out).reshape(B, T, D), (C.astype(C0.dtype), n.astype(n0.dtype), m.astype(m0.dtype))


RWKV_TC = 32
RWKV_ROWS = 8
RWKV_LANES = 128


def _rwkv_scan_kernel(r_ref, k_ref, v_ref, wp_ref, ap_ref, par_ref, s0_ref, y_ref, sout_ref,
                      s_scr, w_scr, b_scr, kx_scr, nkk_scr, y_scr, *, tc, nchunks, rows):
    c = pl.program_id(1)

    @pl.when(c == 0)
    def _():
        s_scr[...] = s0_ref[0]

    def prep(t, carry):
        kt = k_ref[0, t]
        z = -wp_ref[0, t]
        softplus = jnp.maximum(z, 0.0) + jnp.log(1.0 + jnp.exp(-jnp.abs(z)))
        w_scr[t] = jnp.exp(-jnp.exp(-softplus - 0.5))
        a = 1.0 / (1.0 + jnp.exp(-ap_ref[0, t]))
        kk = kt * par_ref[0]
        kk = kk / jnp.maximum(jnp.sqrt(jnp.sum(kk * kk, axis=0, keepdims=True)), 1e-12)
        b_scr[t] = kk * a
        nkk_scr[t] = -kk
        kx_scr[t] = kt * (1.0 + (a - 1.0) * par_ref[1])
        return carry

    lax.fori_loop(0, tc, prep, 0)

    def vblock(vb, carry):
        v0 = vb * rows

        def step(t, inner):
            for i in range(rows):
                S = s_scr[v0 + i]
                sa = jnp.sum(S * nkk_scr[t], axis=0, keepdims=True)
                vrow = v_ref[0, t, pl.ds(v0 + i, 1), :]
                S = S * w_scr[t] + sa * b_scr[t] + vrow * kx_scr[t]
                s_scr[v0 + i] = S
                y_scr[t, pl.ds(v0 + i, 1), :] = jnp.sum(S * r_ref[0, t], axis=0, keepdims=True)
            return inner

        lax.fori_loop(0, tc, step, 0)
        return carry

    lax.fori_loop(0, R_HEAD // rows, vblock, 0)

    def post(t, carry):
        y = y_scr[t]
        mu = jnp.mean(y, axis=0, keepdims=True)
        var = jnp.mean((y - mu) ** 2, axis=0, keepdims=True)
        yn = ((y - mu) * lax.rsqrt(var + R_GN_EPS)) * par_ref[3] + par_ref[4]
        bonus = jnp.sum(r_ref[0, t] * kx_scr[t] * par_ref[2], axis=0, keepdims=True) * v_ref[0, t]
        y_ref[0, t] = yn + bonus
        return carry

    lax.fori_loop(0, tc, post, 0)

    @pl.when(c == nchunks - 1)
    def _():
        sout_ref[0] = s_scr[...]


def _to_chain(z, B, T):
    z = z.reshape(B // 2, 2, T, R_HEADS, R_HEAD)
    return jnp.transpose(z, (0, 2, 4, 1, 3)).reshape(B // 2, T, R_HEAD, RWKV_LANES)


def _vec_to_chain(vec):
    return jnp.tile(vec.reshape(R_HEADS, R_HEAD).T, (1, 2))


def rwkv_scan(r, k, v, w_pre, a_pre, S0, k_k, k_a, r_k, gn_w, gn_b):
    B, T, D = r.shape
    assert B % 2 == 0 and 2 * R_HEADS == RWKV_LANES and T % RWKV_TC == 0
    G = B // 2
    nchunks = T // RWKV_TC
    f32 = jnp.float32
    ins = [_to_chain(z.astype(f32), B, T) for z in (r, k, v, w_pre, a_pre)]
    par = jnp.stack([_vec_to_chain(p.astype(f32)) for p in (k_k, k_a, r_k, gn_w, gn_b)])
    s0 = jnp.transpose(S0.astype(f32).reshape(G, 2, R_HEADS, R_HEAD, R_HEAD), (0, 3, 4, 1, 2))
    s0 = s0.reshape(G, R_HEAD, R_HEAD, RWKV_LANES)
    seq_spec = pl.BlockSpec((1, RWKV_TC, R_HEAD, RWKV_LANES), lambda g, c: (g, c, 0, 0))
    st_spec = pl.BlockSpec((1, R_HEAD, R_HEAD, RWKV_LANES), lambda g, c: (g, 0, 0, 0))
    seq_scr = pltpu.VMEM((RWKV_TC, R_HEAD, RWKV_LANES), f32)
    y, s_out = pl.pallas_call(
        functools.partial(_rwkv_scan_kernel, tc=RWKV_TC, nchunks=nchunks, rows=RWKV_ROWS),
        grid=(G, nchunks),
        in_specs=[seq_spec] * 5 + [pl.BlockSpec((5, R_HEAD, RWKV_LANES), lambda g, c: (0, 0, 0)), st_spec],
        out_specs=[seq_spec, st_spec],
        out_shape=[jax.ShapeDtypeStruct((G, T, R_HEAD, RWKV_LANES), f32),
                   jax.ShapeDtypeStruct((G, R_HEAD, R_HEAD, RWKV_LANES), f32)],
        scratch_shapes=[pltpu.VMEM((R_HEAD, R_HEAD, RWKV_LANES), f32)] + [seq_scr] * 5,
        compiler_params=pltpu.CompilerParams(
            dimension_semantics=("parallel", "arbitrary"),
            vmem_limit_bytes=VMEM_LIMIT_BYTES),
        name="rwkv_scan",
    )(*ins, par, s0)
    y = jnp.transpose(y.reshape(G, T, R_HEAD, 2, R_HEADS), (0, 3, 1, 4, 2)).reshape(B, T, D)
    s_out = jnp.transpose(s_out.reshape(G, R_HEAD, R_HEAD, 2, R_HEADS), (0, 3, 4, 1, 2))
    return y, s_out.reshape(B, R_HEADS, R_HEAD, R_HEAD)


def rwkv7_mixer(x, S0, shift0, mix, vec, w_rkv, w1, w2, a1, a2, g1, g2, wo, use_scan_kernel):
    B, T, D = x.shape
    f32 = jnp.float32
    bf16 = jnp.bfloat16
    xx = jnp.concatenate([shift0[:, None, :].astype(x.dtype), x[:, :-1]], axis=1) - x

    def lerp(j):
        return (x + xx * mix[j]).astype(bf16)

    w0, a0, k_k, k_a, r_k, gn_w, gn_b = vec[0], vec[1], vec[2], vec[3], vec[4], vec[5], vec[6]
    r = mm(lerp(0), w_rkv, layer=0)
    k = mm(lerp(2), w_rkv, layer=1)
    v = mm(lerp(3), w_rkv, layer=2)
    w_pre = (w0 + mm(jnp.tanh(mm(lerp(1), w1)).astype(bf16), w2)).astype(f32)
    a_pre = (a0 + mm(mm(lerp(4), a1).astype(bf16), a2)).astype(f32)
    g = mm(jax.nn.sigmoid(mm(lerp(5), g1)).astype(bf16), g2)
    if use_scan_kernel:
        y, S = rwkv_scan(r, k, v, w_pre, a_pre, S0, k_k, k_a, r_k, gn_w, gn_b)
        return mm((y * g).astype(bf16), wo), (S.astype(S0.dtype), x[:, -1])

    w_log = -jax.nn.softplus(-w_pre) - 0.5
    decay = jnp.exp(-jnp.exp(w_log))
    a = jax.nn.sigmoid(a_pre)

    def heads(z):
        return z.reshape(B, T, R_HEADS, R_HEAD).astype(f32)

    kk = heads(k * k_k)
    kk = kk / jnp.maximum(jnp.sqrt(jnp.sum(kk * kk, axis=-1, keepdims=True)), 1e-12)
    k = k.astype(f32) * (1.0 + (a - 1.0) * k_a)
    r_h, k_h, v_h, a_h, w_h = heads(r), heads(k), heads(v), heads(a), heads(decay)

    def step(S, inp):
        r_t, w_t, k_t, v_t, kk_t, a_t = inp
        sa = jnp.sum(S * (-kk_t)[:, :, None, :], axis=-1)
        S = S * w_t[:, :, None, :] + sa[..., None] * (kk_t * a_t)[:, :, None, :] + v_t[..., None] * k_t[:, :, None, :]
        return S, jnp.sum(S * r_t[:, :, None, :], axis=-1)

    S, y = lax.scan(step, S0.astype(f32),
                    (jnp.moveaxis(r_h, 1, 0), jnp.moveaxis(w_h, 1, 0), jnp.moveaxis(k_h, 1, 0),
                     jnp.moveaxis(v_h, 1, 0), jnp.moveaxis(kk, 1, 0), jnp.moveaxis(a_h, 1, 0)))
    y = jnp.moveaxis(y, 0, 1)
    mu = jnp.mean(y, axis=-1, keepdims=True)
    var = jnp.mean((y - mu) ** 2, axis=-1, keepdims=True)
    y = ((y - mu) * lax.rsqrt(var + R_GN_EPS)).reshape(B, T, D) * gn_w + gn_b
    bonus = jnp.sum(r_h * k_h * r_k.reshape(R_HEADS, R_HEAD), axis=-1, keepdims=True) * v_h
    y = (y + bonus.reshape(B, T, D)) * g
    return mm(y.astype(bf16), wo), (S.astype(S0.dtype), x[:, -1])


def shortconv_mixer(x, buf, w_in, conv_w, w_out):
    T = x.shape[1]
    b_gate, c_gate, u = jnp.split(mm(x, w_in), 3, axis=-1)
    z = jnp.concatenate([buf.astype(u.dtype), c_gate * u], axis=1)
    conv = sum(z[:, j:j + T] * conv_w[j] for j in range(CONV_W))
    return mm(b_gate * conv, w_out), z[:, T:]


NSA_QB = 128
NSA_KT = 512
NSA_WT = WINDOW + NSA_QB
NSA_SLOTS = 128


def _dot_nt(a, b):
    return lax.dot_general(a, b, (((1,), (1,)), ((), ())), preferred_element_type=jnp.float32)


def _softmax_rows(s, mask):
    s = jnp.where(mask, s, NEG_INF)
    m = jnp.max(s, axis=-1, keepdims=True)
    e = jnp.where(mask, jnp.exp(s - m), 0.0)
    return e / jnp.maximum(jnp.sum(e, axis=-1, keepdims=True), TINY)


def _nsa_attn_kernel(q_ref, ks_ref, vs_ref, kw_ref, vw_ref, kc_ref, vc_ref, gate_ref, cos_ref, sna_ref, snb_ref,
                     ovl_ref, exp_ref, o_ref, *, qb, seq, n_cmp, n_sel):
    f32, bf16 = jnp.float32, jnp.bfloat16
    t0 = pl.program_id(2) * qb
    scale = A_HD ** -0.5
    cos, sna, snb = cos_ref[...], sna_ref[...], snb_ref[...]

    def rope(xh):
        return xh * cos + pltpu.roll(xh, A_HD - ROPE_DIM // 2, 1) * sna + pltpu.roll(xh, ROPE_DIM // 2, 1) * snb

    heads = [q_ref[0, :, h * A_HD:(h + 1) * A_HD] for h in range(A_HPG)]
    qc = jnp.concatenate([xh * scale for xh in heads], axis=0)
    qs = jnp.concatenate([(rope(xh) * scale).astype(bf16) for xh in heads], axis=0)
    rows = A_HPG * qb
    tq = t0 + lax.broadcasted_iota(jnp.int32, (qb, 1), 0)
    tq3 = tq[None]

    lane = lax.broadcasted_iota(jnp.int32, (1, 1, NSA_SLOTS), 2)
    kc = kc_ref[0]
    qc_hi, kc_hi = qc.astype(bf16), kc.astype(bf16)
    qc_lo, kc_lo = (qc - qc_hi.astype(f32)).astype(bf16), (kc - kc_hi.astype(f32)).astype(bf16)
    s_c = (_dot_nt(qc_hi, kc_hi) + _dot_nt(qc_hi, kc_lo) + _dot_nt(qc_lo, kc_hi)).reshape(A_HPG, qb, NSA_SLOTS)
    c_mask = (lane * CMP_STRIDE + (CMP_LEN - 1) <= tq3) & (lane < n_cmp)
    p_c = _softmax_rows(s_c, c_mask)
    o_c = jnp.dot(p_c.reshape(rows, NSA_SLOTS).astype(bf16), vc_ref[0].astype(bf16), preferred_element_type=f32)
    psum = jnp.sum(p_c, axis=0)
    ovl = ovl_ref[...]
    hi = psum.astype(bf16)
    r1 = psum - hi.astype(f32)
    mid = r1.astype(bf16)
    lo = (r1 - mid.astype(f32)).astype(bf16)
    imp = (jnp.dot(hi, ovl, preferred_element_type=f32) + jnp.dot(mid, ovl, preferred_element_type=f32)
           + jnp.dot(lo, ovl, preferred_element_type=f32))

    sidx = lax.broadcasted_iota(jnp.int32, (qb, NSA_SLOTS), 1)
    cur = lax.shift_right_logical(tq, SEL_BLK.bit_length() - 1)
    valid = (sidx * SEL_BLK <= tq) & (sidx < n_sel)
    forced = (sidx == 0) | (sidx == cur) | (sidx == cur - 1)
    score = jnp.where(valid, jnp.where(forced, FORCE_SCORE, imp), NEG_INF)
    rank = jnp.zeros((qb, NSA_SLOTS), jnp.int32)
    for j in range(n_sel):
        cj = score[:, j:j + 1]
        beats = (cj > score) | ((cj == score) & (sidx > j))
        rank = rank + beats.astype(jnp.int32)
    sel = ((rank < SEL_TOP) & valid).astype(bf16)

    kpos0 = lax.broadcasted_iota(jnp.int32, (qb, NSA_KT), 1)

    def sel_tile(kt, carry):
        m, l, acc = carry
        k0 = pl.multiple_of(kt * NSA_KT, NSA_KT)
        s = _dot_nt(qs, ks_ref[0, pl.ds(k0, NSA_KT), :].astype(bf16)).reshape(A_HPG, qb, NSA_KT)
        picked = jnp.dot(sel, exp_ref[kt], preferred_element_type=f32) > 0.5
        mask = (picked & (kpos0 + k0 <= tq))[None]
        s = jnp.where(mask, s, NEG_INF)
        m_new = jnp.maximum(m, jnp.max(s, axis=-1, keepdims=True))
        alpha = jnp.exp(m - m_new)
        e = jnp.where(mask, jnp.exp(s - m_new), 0.0)
        l = alpha * l + jnp.sum(e, axis=-1, keepdims=True)
        pv = jnp.dot(e.reshape(rows, NSA_KT).astype(bf16), vs_ref[0, pl.ds(k0, NSA_KT), :].astype(bf16),
                     preferred_element_type=f32)
        acc = alpha * acc + pv.reshape(A_HPG, qb, A_HD)
        return m_new, l, acc

    n_tiles = (t0 + qb + NSA_KT - 1) // NSA_KT
    m0 = jnp.full((A_HPG, qb, 1), NEG_INF, f32)
    _, l_s, acc_s = lax.fori_loop(0, n_tiles, sel_tile,
                                  (m0, jnp.zeros((A_HPG, qb, 1), f32), jnp.zeros((A_HPG, qb, A_HD), f32)))
    o_s = acc_s / jnp.maximum(l_s, TINY)

    w0 = pl.multiple_of(jnp.clip(t0 + qb - NSA_WT, 0, seq - NSA_WT), qb)
    s_w = _dot_nt(qs, kw_ref[0, pl.ds(w0, NSA_WT), :].astype(bf16)).reshape(A_HPG, qb, NSA_WT)
    wpos = w0 + lax.broadcasted_iota(jnp.int32, (1, 1, NSA_WT), 2)
    w_mask = (wpos <= tq3) & (wpos > tq3 - WINDOW)
    p_w = _softmax_rows(s_w, w_mask)
    o_w = jnp.dot(p_w.reshape(rows, NSA_WT).astype(bf16), vw_ref[0, pl.ds(w0, NSA_WT), :].astype(bf16),
                  preferred_element_type=f32).reshape(A_HPG, qb, A_HD)

    o_c = o_c.reshape(A_HPG, qb, A_HD)
    gate = 1.0 / (1.0 + jnp.exp(-gate_ref[0, 0]))
    for h in range(A_HPG):
        o_ref[0, :, h * A_HD:(h + 1) * A_HD] = (gate[:, 3 * h:3 * h + 1] * o_c[h]
                                                + gate[:, 3 * h + 1:3 * h + 2] * o_s[h]
                                                + gate[:, 3 * h + 2:3 * h + 3] * o_w[h])


def nsa_attend_prompt(q, rows, win, kc, vc, gate_logits):
    B, T, _ = q.shape
    f32, bf16 = jnp.float32, jnp.bfloat16
    assert T % NSA_QB == 0 and T % NSA_KT == 0 and T >= NSA_WT and T % SEL_BLK == 0
    n_cmp = kc.shape[1]
    n_sel = T // SEL_BLK
    assert n_cmp <= NSA_SLOTS and n_sel <= NSA_SLOTS
    pad = ((0, 0), (0, NSA_SLOTS - n_cmp), (0, 0), (0, 0))
    kc_p = jnp.pad(kc, pad).reshape(B, NSA_SLOTS, A_KV * A_HD)
    vc_p = jnp.pad(vc, pad).reshape(B, NSA_SLOTS, A_KV * A_HD)
    gate_g = jnp.transpose(gate_logits.reshape(B, T, A_KV, 3 * A_HPG), (0, 2, 1, 3)).astype(f32)
    half = ROPE_DIM // 2
    inv = ROPE_THETA ** (-jnp.arange(half, dtype=f32) / half)
    ang = jnp.arange(T, dtype=jnp.int32).astype(f32)[:, None] * inv[None, :]
    zeros = jnp.zeros((T, A_HD - ROPE_DIM), f32)
    cos_t = jnp.concatenate([jnp.cos(ang), jnp.cos(ang), 1.0 + zeros], axis=1)
    sna_t = jnp.concatenate([-jnp.sin(ang), jnp.zeros((T, half), f32), zeros], axis=1)
    snb_t = jnp.concatenate([jnp.zeros((T, half), f32), jnp.sin(ang), zeros], axis=1)
    c_start = jnp.arange(NSA_SLOTS) * CMP_STRIDE
    s_start = jnp.arange(NSA_SLOTS) * SEL_BLK
    ovl = ((c_start[:, None] < s_start[None, :] + SEL_BLK) & (c_start[:, None] + CMP_LEN > s_start[None, :])
           & (jnp.arange(NSA_SLOTS)[:, None] < n_cmp) & (jnp.arange(NSA_SLOTS)[None, :] < n_sel)).astype(bf16)
    expand = (jnp.arange(NSA_SLOTS)[:, None] == (jnp.arange(T)[None, :] // SEL_BLK)).astype(bf16)
    expand = jnp.transpose(expand.reshape(NSA_SLOTS, T // NSA_KT, NSA_KT), (1, 0, 2))

    def kv_spec(col0):
        return pl.BlockSpec((1, T, A_HD), lambda b, g, i: (b, 0, col0 + g))

    tab_spec = pl.BlockSpec((NSA_QB, A_HD), lambda b, g, i: (i, 0))
    return pl.pallas_call(
        functools.partial(_nsa_attn_kernel, qb=NSA_QB, seq=T, n_cmp=n_cmp, n_sel=n_sel),
        grid=(B, A_KV, T // NSA_QB),
        in_specs=[pl.BlockSpec((1, NSA_QB, A_HPG * A_HD), lambda b, g, i: (b, i, g)),
                  kv_spec(2 * A_KV), kv_spec(3 * A_KV), kv_spec(0), kv_spec(A_KV),
                  pl.BlockSpec((1, NSA_SLOTS, A_HD), lambda b, g, i: (b, 0, g)),
                  pl.BlockSpec((1, NSA_SLOTS, A_HD), lambda b, g, i: (b, 0, g)),
                  pl.BlockSpec((1, 1, NSA_QB, 3 * A_HPG), lambda b, g, i: (b, g, i, 0)),
                  tab_spec, tab_spec, tab_spec,
                  pl.BlockSpec((NSA_SLOTS, NSA_SLOTS), lambda b, g, i: (0, 0)),
                  pl.BlockSpec((T // NSA_KT, NSA_SLOTS, NSA_KT), lambda b, g, i: (0, 0, 0))],
        out_specs=pl.BlockSpec((1, NSA_QB, A_HPG * A_HD), lambda b, g, i: (b, i, g)),
        out_shape=jax.ShapeDtypeStruct((B, T, A_HEADS * A_HD), f32),
        compiler_params=pltpu.CompilerParams(
            dimension_semantics=("parallel", "parallel", "arbitrary"),
            vmem_limit_bytes=VMEM_LIMIT_BYTES),
        name="nsa_attn_prompt",
    )(q, rows, rows, win, win, kc_p, vc_p, gate_g, cos_t, sna_t, snb_t, ovl, expand)


def nsa_project(x, pos, w_in):
    B, T, _ = x.shape
    qd, kvd = A_HEADS * A_HD, 6 * A_KV * A_HD
    q, kv, gate = jnp.split(mm(x, w_in), [qd, qd + kvd], axis=-1)
    q = q.reshape(B, T, A_HEADS, A_HD)
    kv = kv.reshape(B, T, 6, A_KV, A_HD)
    rows = jnp.stack([kv[:, :, 0], kv[:, :, 1], partial_rope(kv[:, :, 2], pos), kv[:, :, 3]], axis=2)
    win = jnp.stack([partial_rope(kv[:, :, 4], pos), kv[:, :, 5]], axis=2)
    gate = jax.nn.sigmoid(gate.reshape(B, T, A_HEADS, 3).astype(jnp.float32))
    return q, partial_rope(q, pos), rows, win, gate


def nsa_compress(kv_cmp, pe, w1, w2):
    B, T = kv_cmp.shape[:2]
    nc = (T - CMP_LEN) // CMP_STRIDE + 1
    idx = (jnp.arange(nc) * CMP_STRIDE)[:, None] + jnp.arange(CMP_LEN)[None, :]
    blk = kv_cmp[:, idx] + jnp.transpose(pe, (1, 0, 2))[:, :, None, :]
    flat = jnp.transpose(blk, (0, 1, 3, 4, 2, 5)).reshape(B, nc, 2, A_KV, CMP_LEN * A_HD)
    hid = jax.nn.gelu(jnp.einsum('bnckf,cfm->bnckm', flat, w1))
    out = jnp.einsum('bnckm,cmd->bnckd', hid, w2)
    return out[:, :, 0], out[:, :, 1]


def nsa_prompt(x, w_in, pe, w1, w2, w_out):
    B, T, _ = x.shape
    pos = jnp.arange(T, dtype=jnp.int32)
    qd, kvd = A_HEADS * A_HD, 6 * A_KV * A_HD
    q, kv, gate_logits = jnp.split(mm(x, w_in), [qd, qd + kvd], axis=-1)
    kv = kv.reshape(B, T, 6, A_KV, A_HD)
    rows = jnp.stack([kv[:, :, 0], kv[:, :, 1], partial_rope(kv[:, :, 2], pos), kv[:, :, 3]], axis=2)
    win = jnp.stack([partial_rope(kv[:, :, 4], pos), kv[:, :, 5]], axis=2)
    kc, vc = nsa_compress(rows[:, :, 0:2], pe, w1, w2)
    o = nsa_attend_prompt(q, rows.reshape(B, T, 4 * A_KV * A_HD), win.reshape(B, T, 2 * A_KV * A_HD), kc, vc,
                          gate_logits)
    return mm(o, w_out), rows, win[:, T - min(WINDOW, T):]


CMP_PER_PAGE = PAGE_SIZE // CMP_STRIDE


def _nsa_cmp_paged_kernel(pt_ref, pa_ref, pb_ref, perm_ref, pe_ref, w1_ref, w2_ref, o_ref, ys_scr, *, npairs,
                          rows_per_g):
    f32, bf16 = jnp.float32, jnp.bfloat16
    p2 = pl.program_id(1)
    for c in range(2):
        for g in range(A_KV):
            xa = jnp.dot(perm_ref[...], pa_ref[0, :, c, g, :].astype(bf16), preferred_element_type=f32)
            xb = jnp.dot(perm_ref[...], pb_ref[0, :, c, g, :].astype(bf16), preferred_element_type=f32)
            row0 = pl.multiple_of(g * rows_per_g + p2 * (2 * CMP_PER_PAGE), 2 * CMP_PER_PAGE)
            for l in range(CMP_STRIDE):
                r0 = l * CMP_PER_PAGE
                ys_scr[c, pl.ds(row0, 2 * CMP_PER_PAGE), l * A_HD:(l + 1) * A_HD] = jnp.concatenate(
                    [xa[r0:r0 + CMP_PER_PAGE], xb[r0:r0 + CMP_PER_PAGE]], axis=0).astype(bf16)

    @pl.when(p2 == npairs - 1)
    def _():
        half = CMP_STRIDE * A_HD
        for c in range(2):
            w1 = w1_ref[c].astype(bf16)
            ys = ys_scr[c]
            first = jnp.dot(ys, w1[:half], preferred_element_type=f32)
            second = jnp.dot(ys, w1[half:], preferred_element_type=f32)
            pe_rows = jnp.broadcast_to(pe_ref[c], (8, CMP_LEN * A_HD)).astype(bf16)
            bias = jnp.dot(pe_rows, w1, preferred_element_type=f32)[0:1]
            hid = jax.nn.gelu(first + pltpu.roll(second, ys.shape[0] - 1, 0) + bias)
            o_ref[0, c] = jnp.dot(hid.astype(bf16), w2_ref[c].astype(bf16), preferred_element_type=f32)


def nsa_compress_paged(cache2, page_table, pe, w1, w2):
    B, P = page_table.shape
    assert P % 2 == 0
    rows_per_g = P * CMP_PER_PAGE
    src = jnp.arange(PAGE_SIZE)
    perm = (jnp.arange(PAGE_SIZE)[None, :]
            == ((src % CMP_PER_PAGE) * CMP_STRIDE + src // CMP_PER_PAGE)[:, None]).astype(jnp.bfloat16)

    def page_spec(which):
        return pl.BlockSpec((1, PAGE_SIZE, 2, A_KV, A_HD), lambda b, p, pt: (pt[b * P + 2 * p + which], 0, 0, 0, 0))

    grid_spec = pltpu.PrefetchScalarGridSpec(
        num_scalar_prefetch=1,
        grid=(B, P // 2),
        in_specs=[page_spec(0), page_spec(1),
                  pl.BlockSpec((PAGE_SIZE, PAGE_SIZE), lambda b, p, pt: (0, 0)),
                  pl.BlockSpec((2, 1, CMP_LEN * A_HD), lambda b, p, pt: (0, 0, 0)),
                  pl.BlockSpec(w1.shape, lambda b, p, pt: (0, 0, 0), pipeline_mode=pl.Buffered(1)),
                  pl.BlockSpec(w2.shape, lambda b, p, pt: (0, 0, 0))],
        out_specs=pl.BlockSpec((1, 2, A_KV * rows_per_g, A_HD), lambda b, p, pt: (b, 0, 0, 0)),
        scratch_shapes=[pltpu.VMEM((2, A_KV * rows_per_g, CMP_STRIDE * A_HD), jnp.bfloat16)])
    return pl.pallas_call(
        functools.partial(_nsa_cmp_paged_kernel, npairs=P // 2, rows_per_g=rows_per_g),
        grid_spec=grid_spec,
        out_shape=jax.ShapeDtypeStruct((B, 2, A_KV * rows_per_g, A_HD), jnp.float32),
        compiler_params=pltpu.CompilerParams(
            dimension_semantics=("parallel", "arbitrary"),
            vmem_limit_bytes=VMEM_LIMIT_BYTES),
        name="nsa_compress_paged",
    )(page_table.reshape(-1).astype(jnp.int32), cache2, cache2, perm, pe.reshape(2, 1, CMP_LEN * A_HD), w1, w2)


def _nsa_decode_kernel(pt_ref, qc_ref, qs_ref, gate_ref, new_ref, kcv_ref, win_ref, ovl_ref, exp_ref, pa_ref, pb_ref,
                       o_ref, sel_scr, oc_scr, ow_scr, m_scr, l_scr, acc_scr, *, npairs, n_cmp, n_sel, past, slots):
    f32, bf16 = jnp.float32, jnp.bfloat16
    p2 = pl.program_id(1)
    rows_per_g = kcv_ref.shape[2] // A_KV
    wlen = win_ref.shape[1]

    @pl.when(p2 == 0)
    def _():
        lane = lax.broadcasted_iota(jnp.int32, (1, rows_per_g), 1)
        c_mask = (lane * CMP_STRIDE + (CMP_LEN - 1) <= past) & (lane < n_cmp)
        sidx = lax.broadcasted_iota(jnp.int32, (1, slots), 1)
        jidx = lax.broadcasted_iota(jnp.int32, (slots, slots), 0)
        cur = past // SEL_BLK
        valid = (sidx * SEL_BLK <= past) & (sidx < n_sel)
        forced = (sidx == 0) | (sidx == cur) | (sidx == cur - 1)
        wpos = past - wlen + lax.broadcasted_iota(jnp.int32, (1, wlen), 1)
        w_mask = (wpos <= past) & (wpos > past - WINDOW) & (wpos >= 0)
        for g in range(A_KV):
            qc = qc_ref[0, g]
            kc = kcv_ref[0, 0, g * rows_per_g:(g + 1) * rows_per_g, :]
            vc = kcv_ref[0, 1, g * rows_per_g:(g + 1) * rows_per_g, :]
            qc_hi, kc_hi = qc.astype(bf16), kc.astype(bf16)
            qc_lo, kc_lo = (qc - qc_hi.astype(f32)).astype(bf16), (kc - kc_hi.astype(f32)).astype(bf16)
            s_c = _dot_nt(qc_hi, kc_hi) + _dot_nt(qc_hi, kc_lo) + _dot_nt(qc_lo, kc_hi)
            p_c = _softmax_rows(s_c, c_mask)
            oc_scr[g] = jnp.dot(p_c.astype(bf16), vc.astype(bf16), preferred_element_type=f32)
            psum = jnp.broadcast_to(jnp.sum(p_c, axis=0, keepdims=True), (8, rows_per_g))
            hi = psum.astype(bf16)
            r1 = psum - hi.astype(f32)
            mid = r1.astype(bf16)
            lo = (r1 - mid.astype(f32)).astype(bf16)
            ovl = ovl_ref[...]
            imp = (jnp.dot(hi, ovl, preferred_element_type=f32) + jnp.dot(mid, ovl, preferred_element_type=f32)
                   + jnp.dot(lo, ovl, preferred_element_type=f32))
            score = jnp.where(valid, jnp.where(forced, FORCE_SCORE, imp), NEG_INF)
            col = score.T[:, 0:1]
            row = score[0:1]
            beats = (col > row) | ((col == row) & (jidx < sidx))
            rank = jnp.sum(beats.astype(jnp.int32), axis=0, keepdims=True)
            sel_scr[g] = jnp.broadcast_to(((rank < SEL_TOP) & valid).astype(f32), (8, slots))
            qs = qs_ref[0, g]
            kw = win_ref[0, :, g * A_HD:(g + 1) * A_HD]
            vw = win_ref[0, :, (A_KV + g) * A_HD:(A_KV + g + 1) * A_HD]
            s_w = jnp.where(w_mask, _dot_nt(qs.astype(bf16), kw.astype(bf16)), NEG_INF)
            s_n = jnp.sum(qs.astype(bf16).astype(f32) * new_ref[0, 2, g].astype(bf16).astype(f32), axis=1, keepdims=True)
            m_w = jnp.maximum(jnp.max(s_w, axis=1, keepdims=True), s_n)
            e_w = jnp.where(w_mask, jnp.exp(s_w - m_w), 0.0)
            e_n = jnp.exp(s_n - m_w)
            den = jnp.maximum(jnp.sum(e_w, axis=1, keepdims=True) + e_n, TINY)
            ow_scr[g] = (jnp.dot((e_w / den).astype(bf16), vw.astype(bf16), preferred_element_type=f32)
                         + (e_n / den).astype(bf16).astype(f32) * new_ref[0, 3, g].astype(bf16).astype(f32))
        m_scr[...] = jnp.full(m_scr.shape, NEG_INF, f32)
        l_scr[...] = jnp.zeros(l_scr.shape, f32)
        acc_scr[...] = jnp.zeros(acc_scr.shape, f32)

    for g in range(A_KV):
        qs = qs_ref[0, g].astype(bf16)
        k = jnp.concatenate([pa_ref[0, :, 0, g, :], pb_ref[0, :, 0, g, :]], axis=0)
        v = jnp.concatenate([pa_ref[0, :, 1, g, :], pb_ref[0, :, 1, g, :]], axis=0)
        picked = jnp.dot(sel_scr[g].astype(bf16), exp_ref[0], preferred_element_type=f32) > 0.5
        s = jnp.where(picked, _dot_nt(qs, k.astype(bf16)), NEG_INF)
        m_new = jnp.maximum(m_scr[g], jnp.max(s, axis=1, keepdims=True))
        alpha = jnp.exp(m_scr[g] - m_new)
        e = jnp.where(picked, jnp.exp(s - m_new), 0.0)
        l_scr[g] = alpha * l_scr[g] + jnp.sum(e, axis=1, keepdims=True)
        acc_scr[g] = alpha * acc_scr[g] + jnp.dot(e.astype(bf16), v.astype(bf16), preferred_element_type=f32)
        m_scr[g] = m_new

    @pl.when(p2 == npairs - 1)
    def _():
        last = n_sel - 1
        for g in range(A_KV):
            qs = qs_ref[0, g].astype(bf16).astype(f32)
            s_n = jnp.sum(qs * new_ref[0, 0, g].astype(bf16).astype(f32), axis=1, keepdims=True)
            ok = sel_scr[g][:, last:last + 1] > 0.5
            s_n = jnp.where(ok, s_n, NEG_INF)
            m_new = jnp.maximum(m_scr[g], s_n)
            alpha = jnp.exp(m_scr[g] - m_new)
            e_n = jnp.where(ok, jnp.exp(s_n - m_new), 0.0)
            l_fin = alpha * l_scr[g] + e_n
            acc = alpha * acc_scr[g] + e_n.astype(bf16).astype(f32) * new_ref[0, 1, g].astype(bf16).astype(f32)
            o_s = acc / jnp.maximum(l_fin, TINY)
            gt = gate_ref[0, g]
            o_ref[0, g] = gt[:, 0:1] * oc_scr[g] + gt[:, 1:2] * o_s + gt[:, 2:3] * ow_scr[g]


def nsa_decode_attend(q, q_rot, gate, rows_new, win_new, kcv, cache2, page_table, win_buf):
    B, P = page_table.shape
    f32, bf16 = jnp.float32, jnp.bfloat16
    past = P * PAGE_SIZE
    n_cmp = (past + 1 - CMP_LEN) // CMP_STRIDE + 1
    n_sel = -(-(past + 1) // SEL_BLK)
    rows_per_g = P * CMP_PER_PAGE
    slots = -(-n_sel // 128) * 128
    assert n_sel == past // SEL_BLK + 1 and n_cmp <= rows_per_g and P % 2 == 0
    scale = A_HD ** -0.5
    qc = (q.astype(f32) * scale).reshape(B, A_KV, A_HPG, A_HD)
    qs = (q_rot.astype(f32) * scale).reshape(B, A_KV, A_HPG, A_HD)
    gt = gate.reshape(B, A_KV, A_HPG, 3)
    new = jnp.stack([rows_new[:, 0, 2], rows_new[:, 0, 3], win_new[:, 0, 0], win_new[:, 0, 1]], axis=1)
    new = jnp.broadcast_to(new[:, :, :, None, :], (B, 4, A_KV, 1, A_HD))
    wlen = win_buf.shape[1]
    win2 = win_buf.reshape(B, wlen, 2 * A_KV * A_HD)
    c_start = jnp.arange(rows_per_g) * CMP_STRIDE
    s_start = jnp.arange(slots) * SEL_BLK
    ovl = ((c_start[:, None] < s_start[None, :] + SEL_BLK) & (c_start[:, None] + CMP_LEN > s_start[None, :])
           & (jnp.arange(rows_per_g)[:, None] < n_cmp) & (jnp.arange(slots)[None, :] < n_sel)).astype(bf16)
    keys = jnp.arange(past) // SEL_BLK
    expand = (jnp.arange(slots)[:, None] == keys[None, :]).astype(bf16)
    expand = jnp.transpose(expand.reshape(slots, P // 2, 2 * PAGE_SIZE), (1, 0, 2))

    def page_spec(which):
        return pl.BlockSpec((1, PAGE_SIZE, 2, A_KV, A_HD), lambda b, p, pt: (pt[b * P + 2 * p + which], 0, 1, 0, 0))

    def per_b(shape):
        nd = len(shape)
        return pl.BlockSpec((1,) + tuple(shape[1:]), lambda b, p, pt: (b,) + (0,) * (nd - 1))

    grid_spec = pltpu.PrefetchScalarGridSpec(
        num_scalar_prefetch=1,
        grid=(B, P // 2),
        in_specs=[per_b(qc.shape), per_b(qs.shape), per_b(gt.shape), per_b(new.shape), per_b(kcv.shape),
                  per_b(win2.shape),
                  pl.BlockSpec(ovl.shape, lambda b, p, pt: (0, 0)),
                  pl.BlockSpec((1, slots, 2 * PAGE_SIZE), lambda b, p, pt: (p, 0, 0)),
                  page_spec(0), page_spec(1)],
        out_specs=pl.BlockSpec((1, A_KV, A_HPG, A_HD), lambda b, p, pt: (b, 0, 0, 0)),
        scratch_shapes=[pltpu.VMEM((A_KV, A_HPG, slots), f32), pltpu.VMEM((A_KV, A_HPG, A_HD), f32),
                        pltpu.VMEM((A_KV, A_HPG, A_HD), f32), pltpu.VMEM((A_KV, A_HPG, 1), f32),
                        pltpu.VMEM((A_KV, A_HPG, 1), f32), pltpu.VMEM((A_KV, A_HPG, A_HD), f32)])
    o = pl.pallas_call(
        functools.partial(_nsa_decode_kernel, npairs=P // 2, n_cmp=n_cmp, n_sel=n_sel, past=past, slots=slots),
        grid_spec=grid_spec,
        out_shape=jax.ShapeDtypeStruct((B, A_KV, A_HPG, A_HD), f32),
        compiler_params=pltpu.CompilerParams(
            dimension_semantics=("parallel", "arbitrary"),
            vmem_limit_bytes=VMEM_LIMIT_BYTES),
        name="nsa_decode",
    )(page_table.reshape(-1).astype(jnp.int32), qc, qs, gt, new, kcv, win2, ovl, expand, cache2, cache2)
    return o.reshape(B, 1, A_HEADS * A_HD)


def nsa_sample(x, cache_kv, page_table, win_buf, w_in, pe, w1, w2, w_out):
    B, S, _ = x.shape
    assert S == 1
    past = page_table.shape[1] * PAGE_SIZE
    pos = past + jnp.arange(S, dtype=jnp.int32)
    q, q_rot, rows, win, gate = nsa_project(x, pos, w_in)
    cache2 = cache_kv
    kcv = nsa_compress_paged(cache2, page_table, pe, w1, w2)
    o = nsa_decode_attend(q, q_rot, gate, rows, win, kcv, cache2, page_table, win_buf)
    kw_all = jnp.concatenate([win_buf.astype(win.dtype), win], axis=1)
    return mm(o, w_out), rows, kw_all[:, S:]


def kernel(x_prompt, x_sample, state_mlstm_C, state_mlstm_n, state_mlstm_m, state_rwkv_S, state_rwkv_shift,
           state_conv, cache_nsa_kv, state_nsa_win, page_table, norm_g, mlp_up, mlp_down,
           m_w_in, m_b_gate, m_norm, m_w_out, r_mix, r_vec, r_w_rkv, r_w1, r_w2, r_a1, r_a2, r_g1, r_g2, r_wo,
           c_w_in, c_conv, c_w_out, a_w_in, a_cmp_pe, a_cmp_w1, a_cmp_w2, a_w_out):
    xp, xs = x_prompt, x_sample
    Bp = xp.shape[0]
    dt = xp.dtype
    depth = norm_g.shape[0]
    bf16 = jnp.bfloat16

    def h_dtype(layer):
        return dt if layer % N_MIXERS == 1 else bf16

    hp, hs = prenorm(xp, norm_g[0, 0], h_dtype(0)), prenorm(xs, norm_g[0, 0], h_dtype(0))
    for i in range(depth):
        kind = i % N_MIXERS
        if kind == 0:
            yp, (mC_p, mn_p, mm_p) = mlstm_prompt(hp, jnp.zeros((Bp, M_HEADS, M_DQK, M_DV), dt),
                                                  jnp.zeros((Bp, M_HEADS, M_DQK), dt), jnp.zeros((Bp, M_HEADS), dt),
                                                  m_w_in, m_b_gate, m_norm, m_w_out)
            ys, (mC_s, mn_s, mm_s) = mlstm_mixer(hs, state_mlstm_C, state_mlstm_n, state_mlstm_m,
                                                 m_w_in, m_b_gate, m_norm, m_w_out)
        elif kind == 1:
            yp, (rS_p, rx_p) = rwkv7_mixer(hp, jnp.zeros((Bp, R_HEADS, R_HEAD, R_HEAD), dt), jnp.zeros((Bp, D_MODEL), dt),
                                           r_mix, r_vec, r_w_rkv, r_w1, r_w2, r_a1, r_a2, r_g1, r_g2, r_wo, True)
            ys, (rS_s, rx_s) = rwkv7_mixer(hs, state_rwkv_S, state_rwkv_shift,
                                           r_mix, r_vec, r_w_rkv, r_w1, r_w2, r_a1, r_a2, r_g1, r_g2, r_wo, False)
        elif kind == 2:
            yp, cv_p = shortconv_mixer(hp, jnp.zeros((Bp, CONV_W - 1, D_MODEL), dt), c_w_in, c_conv, c_w_out)
            ys, cv_s = shortconv_mixer(hs, state_conv, c_w_in, c_conv, c_w_out)
        else:
            yp, kv_p, win_p = nsa_prompt(hp, a_w_in, a_cmp_pe, a_cmp_w1, a_cmp_w2, a_w_out)
            ys, kv_s, win_s = nsa_sample(hs, cache_nsa_kv, page_table, state_nsa_win,
                                         a_w_in, a_cmp_pe, a_cmp_w1, a_cmp_w2, a_w_out)
        xp, hp = residual_norm(xp, yp, norm_g[i, 1], norm_g[i, 2], bf16)
        xs, hs = residual_norm(xs, ys, norm_g[i, 1], norm_g[i, 2], bf16)
        g_next = norm_g[i + 1, 0] if i + 1 < depth else None
        xp, hp = residual_norm(xp, sq_relu_mlp(hp, mlp_up, mlp_down, i), norm_g[i, 3], g_next, h_dtype(i + 1))
        xs, hs = residual_norm(xs, sq_relu_mlp(hs, mlp_up, mlp_down, i), norm_g[i, 3], g_next, h_dtype(i + 1))
    return (xp, xs, mC_p, mC_s, mn_p, mn_s, mm_p, mm_s, rS_p, rS_s, rx_p, rx_s, cv_p, cv_s, kv_p, kv_s, win_p, win_s)
```
